```python
import math
import jax
import jax.numpy as jnp
from jax import lax
import numpy as np

D_MODEL = 2048
BATCH = 2
SEQ = 4096
DEPTH = 4
DEC_BATCH = 32
DEC_SEQ = 8
PAST_LEN = 16384
PAGE_SIZE = 128

MIX_WIDTH = D_MODEL
N_MIXERS = 4
GW = MIX_WIDTH // N_MIXERS
RMS_EPS = 1e-6
CONV_W = 4

SSM_HEADDIM = 64
SSM_HEADS = GW // SSM_HEADDIM
SSM_STATE = 128
SSM_GROUPS = 2
SSM_CHUNK = 128
SSM_BC = SSM_GROUPS * SSM_STATE
SSM_XBC = GW + 2 * SSM_BC
SSM_COLS = GW + SSM_XBC + SSM_HEADS

RWKV_HEADDIM = 64
RWKV_HEADS = GW // RWKV_HEADDIM
RWKV_W_LORA = 64
RWKV_A_LORA = 64
RWKV_G_LORA = 128
RWKV_COLS = 3 * GW + RWKV_W_LORA + RWKV_A_LORA + RWKV_G_LORA
RWKV_GN_EPS = 64e-5

SWA_HEADDIM = 64
SWA_Q_HEADS = GW // SWA_HEADDIM
SWA_KV_HEADS = 2
SWA_KV_WIDTH = SWA_KV_HEADS * SWA_HEADDIM
WINDOW = 128
SWA_BLOCK = 128
ROT_DIM = SWA_HEADDIM // 4
ROPE_THETA = 500000.0
SWA_COLS = GW + 2 * SWA_KV_WIDTH

LRU_BLOCKS = 8
LRU_BLOCK = GW // LRU_BLOCKS
LRU_C = 8.0
LRU_COLS = 2 * GW

OFF_SSM = 0
OFF_RWKV = OFF_SSM + SSM_COLS
OFF_SWA = OFF_RWKV + RWKV_COLS
OFF_LRU = OFF_SWA + SWA_COLS
IN_COLS = OFF_LRU + LRU_COLS

D_FF = 4 * D_MODEL

kernel_name = "hybrid_ssd_rwkv7_swa_rglru_step"


def rms_norm(x, g):
    xf = x.astype(jnp.float32)
    y = xf * lax.rsqrt(jnp.mean(xf * xf, axis=-1, keepdims=True) + RMS_EPS)
    return (y * g.astype(jnp.float32)).astype(x.dtype)


def causal_conv(u, buf, w, b):
    L = u.shape[1]
    full = jnp.concatenate([buf.astype(u.dtype), u], axis=1)
    out = b + full[:, 0:L] * w[0]
    for j in range(1, CONV_W):
        out = out + full[:, j:j + L] * w[j]
    return out, full[:, L:]


def ssd_scan(x, dt, a, bm, cm, h0):
    f32 = jnp.float32
    bsz, L = x.shape[:2]
    Q = math.gcd(L, SSM_CHUNK)
    nc = L // Q

    def chunks(t):
        return t.astype(f32).reshape((bsz, nc, Q) + t.shape[2:])

    x, dt, bm, cm = chunks(x), chunks(dt), chunks(bm), chunks(cm)
    cum = jnp.cumsum(dt * a, axis=2)
    diff = cum[:, :, :, None] - cum[:, :, None, :]
    causal = jnp.tril(jnp.ones((Q, Q), bool))[None, None, :, :, None, None]
    decay = jnp.exp(jnp.where(causal, diff, -jnp.inf))
    cb = jnp.einsum('bctgn,bcsgn->bctsg', cm, bm)
    dx = dt[..., None] * x
    y_diag = jnp.einsum('bctsgh,bcsghp->bctghp', cb[..., None] * decay, dx)
    tail = jnp.exp(cum[:, :, -1:] - cum)
    chunk_states = jnp.einsum('bcsgn,bcsgh,bcsghp->bcghpn', bm, tail, dx)
    chunk_decay = jnp.exp(cum[:, :, -1])

    def step(h, inp):
        st, dec = inp
        return dec[..., None, None] * h + st, h

    h_final, h_in = lax.scan(step, h0.astype(f32),
                             (jnp.moveaxis(chunk_states, 1, 0), jnp.moveaxis(chunk_decay, 1, 0)))
    h_in = jnp.moveaxis(h_in, 0, 1)
    y_off = jnp.einsum('bctgn,bctgh,bcghpn->bctghp', cm, jnp.exp(cum), h_in)
    y = (y_diag + y_off).reshape((bsz, L) + y_diag.shape[3:])
    return y, h_final


def mamba2_mixer(p, conv_buf, h0, conv_w, conv_b, dt_bias, a_log, d_skip, norm_w):
    f32 = jnp.float32
    bsz, L = p.shape[:2]
    hg = SSM_HEADS // SSM_GROUPS
    z = p[..., :GW]
    xbc = p[..., GW:GW + SSM_XBC]
    dt_raw = p[..., GW + SSM_XBC:]
    xbc, new_buf = causal_conv(xbc, conv_buf, conv_w, conv_b)
    xbc = jax.nn.silu(xbc)
    x = xbc[..., :GW].reshape(bsz, L, SSM_GROUPS, hg, SSM_HEADDIM)
    bm = xbc[..., GW:GW + SSM_BC].reshape(bsz, L, SSM_GROUPS, SSM_STATE)
    cm = xbc[..., GW + SSM_BC:].reshape(bsz, L, SSM_GROUPS, SSM_STATE)
    dt = jax.nn.softplus(dt_raw.astype(f32) + dt_bias.astype(f32)).reshape(bsz, L, SSM_GROUPS, hg)
    a = -jnp.exp(a_log.astype(f32)).reshape(SSM_GROUPS, hg)
    y, h_new = ssd_scan(x, dt, a, bm, cm,
                        h0.reshape(bsz, SSM_GROUPS, hg, SSM_HEADDIM, SSM_STATE))
    y = y + d_skip.astype(f32).reshape(SSM_GROUPS, hg)[:, :, None] * x.astype(f32)
    y = y.reshape(bsz, L, GW) * jax.nn.silu(z.astype(f32))
    yg = y.reshape(bsz, L, SSM_GROUPS, GW // SSM_GROUPS)
    yg = yg * lax.rsqrt(jnp.mean(yg * yg, axis=-1, keepdims=True) + RMS_EPS)
    y = yg.reshape(bsz, L, GW) * norm_w.astype(f32)
    h_new = h_new.reshape(bsz, SSM_HEADS, SSM_HEADDIM, SSM_STATE)
    return y.astype(p.dtype), new_buf, h_new.astype(h0.dtype)


def rwkv7_mixer(p, prev, s0, mu, w0, w2, a0, a2, g2, k_k, k_a, r_k, ln_w, ln_b):
    f32 = jnp.float32
    bsz, L = p.shape[:2]
    pf = p.astype(f32)
    shifted = jnp.concatenate([prev.astype(f32)[:, None], pf[:, :-1]], axis=1)
    q = pf + (shifted - pf) * mu.astype(f32)
    o = 3 * GW
    r, k, v = q[..., :GW], q[..., GW:2 * GW], q[..., 2 * GW:o]
    wd = q[..., o:o + RWKV_W_LORA]
    ad = q[..., o + RWKV_W_LORA:o + RWKV_W_LORA + RWKV_A_LORA]
    gd = q[..., o + RWKV_W_LORA + RWKV_A_LORA:]
    w_log = -jax.nn.softplus(-(w0.astype(f32) + jnp.tanh(wd) @ w2.astype(f32))) - 0.5
    decay = jnp.exp(-jnp.exp(w_log))
    a = jax.nn.sigmoid(a0.astype(f32) + ad @ a2.astype(f32))
    g = jax.nn.sigmoid(gd) @ g2.astype(f32)
    kk = k * k_k.astype(f32)
    k = k * (1.0 + (a - 1.0) * k_a.astype(f32))

    def heads(t):
        return t.reshape(bsz, L, RWKV_HEADS, RWKV_HEADDIM)

    r, k, v, kk, a, decay = heads(r), heads(k), heads(v), heads(kk), heads(a), heads(decay)
    kk = kk * lax.rsqrt(jnp.maximum(jnp.sum(kk * kk, axis=-1, keepdims=True), 1e-24))

    def step(S, inp):
        r_t, w_t, k_t, v_t, kk_t, a_t = inp
        sa = jnp.einsum('bhvk,bhk->bhv', S, kk_t)
        S = (S * w_t[:, :, None, :] - sa[..., None] * (kk_t * a_t)[:, :, None, :]
             + v_t[..., None] * k_t[:, :, None, :])
        return S, jnp.einsum('bhvk,bhk->bhv', S, r_t)

    xs = (r, decay, k, v, kk, a)
    s_final, ys = lax.scan(step, s0.astype(f32), tuple(jnp.moveaxis(t, 1, 0) for t in xs))
    y = jnp.moveaxis(ys, 0, 1)
    mean = jnp.mean(y, axis=-1, keepdims=True)
    var = jnp.mean(jnp.square(y - mean), axis=-1, keepdims=True)
    y = ((y - mean) * lax.rsqrt(var + RWKV_GN_EPS)).reshape(bsz, L, GW)
    y = y * ln_w.astype(f32) + ln_b.astype(f32)
    bonus = jnp.sum(r * k * r_k.astype(f32), axis=-1, keepdims=True) * v
    y = (y + bonus.reshape(bsz, L, GW)) * g
    return y.astype(p.dtype), p[:, -1], s_final.astype(s0.dtype)


def rope_partial(x, pos):
    f32 = jnp.float32
    half = ROT_DIM // 2
    inv = ROPE_THETA ** (-(jnp.arange(half, dtype=f32) * 2.0) / ROT_DIM)
    ang = pos.astype(f32)[:, None] * inv
    cos = jnp.cos(ang)[None, :, None, :]
    sin = jnp.sin(ang)[None, :, None, :]
    xf = x.astype(f32)
    x1, x2 = xf[..., :half], xf[..., half:ROT_DIM]
    out = jnp.concatenate([x1 * cos - x2 * sin, x2 * cos + x1 * sin, xf[..., ROT_DIM:]], axis=-1)
    return out.astype(x.dtype)


def swa_mixer(p, k_buf, v_buf, sinks, pos0):
    f32 = jnp.float32
    bsz, L = p.shape[:2]
    grp = SWA_Q_HEADS // SWA_KV_HEADS
    q = p[..., :GW].reshape(bsz, L, SWA_Q_HEADS, SWA_HEADDIM)
    k = p[..., GW:GW + SWA_KV_WIDTH].reshape(bsz, L, SWA_KV_HEADS, SWA_HEADDIM)
    v = p[..., GW + SWA_KV_WIDTH:].reshape(bsz, L, SWA_KV_HEADS, SWA_HEADDIM)
    pos = pos0 + jnp.arange(L)
    q = rope_partial(q, pos)
    k = rope_partial(k, pos)
    k_full = jnp.concatenate([k_buf.astype(k.dtype), k], axis=1)
    v_full = jnp.concatenate([v_buf.astype(v.dtype), v], axis=1)
    k_pos = pos0 - WINDOW + jnp.arange(WINDOW + L)
    qb_len = math.gcd(L, SWA_BLOCK)
    nb = L // qb_len
    idx = (jnp.arange(nb) * qb_len)[:, None] + jnp.arange(WINDOW + qb_len)[None, :]
    kb = k_full[:, idx].astype(f32)
    vb = v_full[:, idx].astype(f32)
    qb = q.reshape(bsz, nb, qb_len, SWA_KV_HEADS, grp, SWA_HEADDIM).astype(f32)
    s = jnp.einsum('bnqhgd,bnkhd->bnhgqk', qb, kb) * (SWA_HEADDIM ** -0.5)
    qp = pos.reshape(nb, qb_len)[:, :, None]
    kp = k_pos[idx][:, None, :]
    valid = (kp >= 0) & (kp <= qp) & (kp >= qp - WINDOW)
    s = jnp.where(valid[None, :, None, None], s, -jnp.inf)
    sink = sinks.astype(f32).reshape(SWA_KV_HEADS, grp)[None, None, :, :, None, None]
    m = jnp.maximum(jnp.max(s, axis=-1, keepdims=True), sink)
    e = jnp.exp(s - m)
    denom = jnp.sum(e, axis=-1, keepdims=True) + jnp.exp(sink - m)
    o = jnp.einsum('bnhgqk,bnkhd->bnqhgd', e / denom, vb)
    return o.reshape(bsz, L, GW).astype(p.dtype), k_full[:, -WINDOW:], v_full[:, -WINDOW:]


def rglru_mixer(p, conv_buf, h0, conv_w, conv_b, wa, ba, wi, bi, lam):
    f32 = jnp.float32
    bsz, L = p.shape[:2]
    xb, gate = p[..., :GW], p[..., GW:]
    xc, new_buf = causal_conv(xb, conv_buf, conv_w, conv_b)
    xf = xc.astype(f32)
    xblk = xf.reshape(bsz, L, LRU_BLOCKS, LRU_BLOCK)
    rg = jax.nn.sigmoid(jnp.einsum('blnc,ncd->blnd', xblk, wa.astype(f32)).reshape(bsz, L, GW) + ba.astype(f32))
    ig = jax.nn.sigmoid(jnp.einsum('blnc,ncd->blnd', xblk, wi.astype(f32)).reshape(bsz, L, GW) + bi.astype(f32))
    log_a = -LRU_C * rg * jax.nn.softplus(-lam.astype(f32))
    a = jnp.exp(log_a)
    b = jnp.sqrt(-jnp.expm1(2.0 * log_a)) * (ig * xf)
    b = b.at[:, 0].add(a[:, 0] * h0.astype(f32))

    def combine(left, right):
        al, bl = left
        ar, br = right
        return al * ar, ar * bl + br

    _, h = lax.associative_scan(combine, (a, b), axis=1)
    y = h * jax.nn.gelu(gate.astype(f32))
    return y.astype(p.dtype), new_buf, h[:, -1].astype(h0.dtype)


def run_trunk(x, pos0, ssm_h, ssm_conv, rwkv_s, rwkv_shift, swa_k, swa_v, lru_h, lru_conv, w):
    names = ('ssm_h', 'ssm_conv', 'rwkv_s', 'rwkv_shift', 'swa_k', 'swa_v', 'lru_h', 'lru_conv')
    new = {n: [] for n in names}
    for l in range(DEPTH):
        h = rms_norm(x, w['norm_mix'][l])
        p = h @ w['w_in'][l]
        y_ssm, c1, h1 = mamba2_mixer(p[..., OFF_SSM:OFF_RWKV], ssm_conv[l], ssm_h[l],
                                     w['ssm_conv_w'][l], w['ssm_conv_b'][l], w['ssm_dt_bias'][l],
                                     w['ssm_a_log'][l], w['ssm_d'][l], w['ssm_norm'][l])
        y_rwkv, sh2, s2 = rwkv7_mixer(p[..., OFF_RWKV:OFF_SWA], rwkv_shift[l], rwkv_s[l],
                                      w['rwkv_mu'][l], w['rwkv_w0'][l], w['rwkv_w2'][l],
                                      w['rwkv_a0'][l], w['rwkv_a2'][l], w['rwkv_g2'][l],
                                      w['rwkv_k_k'][l], w['rwkv_k_a'][l], w['rwkv_r_k'][l],
                                      w['rwkv_ln_w'][l], w['rwkv_ln_b'][l])
        y_swa, k3, v3 = swa_mixer(p[..., OFF_SWA:OFF_LRU], swa_k[l], swa_v[l], w['swa_sinks'][l], pos0)
        y_lru, c4, h4 = rglru_mixer(p[..., OFF_LRU:], lru_conv[l], lru_h[l],
                                    w['lru_conv_w'][l], w['lru_conv_b'][l], w['lru_wa'][l],
                                    w['lru_ba'][l], w['lru_wi'][l], w['lru_bi'][l], w['lru_lambda'][l])
        x = x + jnp.concatenate([y_ssm, y_rwkv, y_swa, y_lru], axis=-1) @ w['w_out'][l]
        h = rms_norm(x, w['norm_mlp'][l])
        x = x + jnp.square(jax.nn.relu(h @ w['mlp_w1'][l])) @ w['mlp_w2'][l]
        for n, val in zip(names, (h1, c1, s2, sh2, k3, v3, h4, c4)):
            new[n].append(val)
    y = rms_norm(x, w['norm_final'])
    return y, {n: jnp.stack(new[n], axis=0) for n in names}


def setup_inputs(seed: int = 0) -> dict:
    key = jax.random.key(seed)
    keys = iter(jax.random.split(key, 64))
    f32 = jnp.float32

    def nrm(shape, scale):
        return jax.random.normal(next(keys), shape, f32) * scale

    def unif(shape, lo, hi):
        return jax.random.uniform(next(keys), shape, f32, lo, hi)

    x_prompt = nrm((BATCH, SEQ, D_MODEL), 1.0)
    x_sample = nrm((DEC_BATCH, DEC_SEQ, D_MODEL), 1.0)
    state_ssm = nrm((DEPTH, DEC_BATCH, SSM_HEADS, SSM_HEADDIM, SSM_STATE), 0.3)
    state_ssm_conv = nrm((DEPTH, DEC_BATCH, CONV_W - 1, SSM_XBC), 1.0)
    state_rwkv = nrm((DEPTH, DEC_BATCH, RWKV_HEADS, RWKV_HEADDIM, RWKV_HEADDIM), 1.0)
    state_rwkv_shift = nrm((DEPTH, DEC_BATCH, RWKV_COLS), 1.0)
    cache_swa_k = nrm((DEPTH, DEC_BATCH, WINDOW, SWA_KV_HEADS, SWA_HEADDIM), 1.0)
    cache_swa_v = nrm((DEPTH, DEC_BATCH, WINDOW, SWA_KV_HEADS, SWA_HEADDIM), 1.0)
    state_lru = nrm((DEPTH, DEC_BATCH, GW), 1.0)
    state_lru_conv = nrm((DEPTH, DEC_BATCH, CONV_W - 1, GW), 1.0)

    dt0 = jnp.exp(unif((DEPTH, SSM_HEADS), math.log(1e-3), math.log(1e-1)))
    a_init = unif((DEPTH, GW), 0.9, 0.999)
    return {
        'x_prompt': x_prompt,
        'x_sample': x_sample,
        'state_ssm': state_ssm,
        'state_ssm_conv': state_ssm_conv,
        'state_rwkv': state_rwkv,
        'state_rwkv_shift': state_rwkv_shift,
        'cache_swa_k': cache_swa_k,
        'cache_swa_v': cache_swa_v,
        'state_lru': state_lru,
        'state_lru_conv': state_lru_conv,
        'norm_mix': 1.0 + nrm((DEPTH, D_MODEL), 0.05),
        'w_in': nrm((DEPTH, D_MODEL, IN_COLS), D_MODEL ** -0.5),
        'ssm_conv_w': nrm((DEPTH, CONV_W, SSM_XBC), 0.5),
        'ssm_conv_b': nrm((DEPTH, SSM_XBC), 0.02),
        'ssm_dt_bias': dt0 + jnp.log(-jnp.expm1(-dt0)),
        'ssm_a_log': jnp.log(unif((DEPTH, SSM_HEADS), 1.0, 16.0)),
        'ssm_d': 1.0 + nrm((DEPTH, SSM_HEADS), 0.1),
        'ssm_norm': 1.0 + nrm((DEPTH, GW), 0.05),
        'rwkv_mu': unif((DEPTH, RWKV_COLS), 0.0, 1.0),
        'rwkv_w0': unif((DEPTH, GW), -3.0, 1.0),
        'rwkv_w2': nrm((DEPTH, RWKV_W_LORA, GW), 0.5 * RWKV_W_LORA ** -0.5),
        'rwkv_a0': nrm((DEPTH, GW), 0.1),
        'rwkv_a2': nrm((DEPTH, RWKV_A_LORA, GW), 0.5 * RWKV_A_LORA ** -0.5),
        'rwkv_g2': nrm((DEPTH, RWKV_G_LORA, GW), RWKV_G_LORA ** -0.5),
        'rwkv_k_k': 0.85 + nrm((DEPTH, GW), 0.05),
        'rwkv_k_a': 1.0 + nrm((DEPTH, GW), 0.05),
        'rwkv_r_k': nrm((DEPTH, RWKV_HEADS, RWKV_HEADDIM), 0.1),
        'rwkv_ln_w': 1.0 + nrm((DEPTH, GW), 0.05),
        'rwkv_ln_b': nrm((DEPTH, GW), 0.02),
        'swa_sinks': nrm((DEPTH, SWA_Q_HEADS), 0.5),
        'lru_conv_w': nrm((DEPTH, CONV_W, GW), 0.5),
        'lru_conv_b': nrm((DEPTH, GW), 0.02),
        'lru_wa': nrm((DEPTH, LRU_BLOCKS, LRU_BLOCK, LRU_BLOCK), LRU_BLOCK ** -0.5),
        'lru_ba': nrm((DEPTH, GW), 0.02),
        'lru_wi': nrm((DEPTH, LRU_BLOCKS, LRU_BLOCK, LRU_BLOCK), LRU_BLOCK ** -0.5),
        'lru_bi': nrm((DEPTH, GW), 0.02),
        'lru_lambda': jnp.log(a_init) - jnp.log1p(-a_init),
        'w_out': nrm((DEPTH, MIX_WIDTH, D_MODEL), MIX_WIDTH ** -0.5),
        'norm_mlp': 1.0 + nrm((DEPTH, D_MODEL), 0.05),
        'mlp_w1': nrm((DEPTH, D_MODEL, D_FF), D_MODEL ** -0.5),
        'mlp_w2': nrm((DEPTH, D_FF, D_MODEL), D_FF ** -0.5),
        'norm_final': 1.0 + nrm((D_MODEL,), 0.05),
    }


def reference(x_prompt, x_sample, state_ssm, state_ssm_conv, state_rwkv, state_rwkv_shift,
              cache_swa_k, cache_swa_v, state_lru, state_lru_conv,
              norm_mix, w_in, ssm_conv_w, ssm_conv_b, ssm_dt_bias, ssm_a_log, ssm_d, ssm_norm,
              rwkv_mu, rwkv_w0, rwkv_w2, rwkv_a0, rwkv_a2, rwkv_g2, rwkv_k_k, rwkv_k_a, rwkv_r_k,
              rwkv_ln_w, rwkv_ln_b, swa_sinks, lru_conv_w, lru_conv_b, lru_wa, lru_ba, lru_wi,
              lru_bi, lru_lambda, w_out, norm_mlp, mlp_w1, mlp_w2, norm_final):
    w = dict(norm_mix=norm_mix, w_in=w_in, ssm_conv_w=ssm_conv_w, ssm_conv_b=ssm_conv_b,
             ssm_dt_bias=ssm_dt_bias, ssm_a_log=ssm_a_log, ssm_d=ssm_d, ssm_norm=ssm_norm,
             rwkv_mu=rwkv_mu, rwkv_w0=rwkv_w0, rwkv_w2=rwkv_w2, rwkv_a0=rwkv_a0, rwkv_a2=rwkv_a2,
             rwkv_g2=rwkv_g2, rwkv_k_k=rwkv_k_k, rwkv_k_a=rwkv_k_a, rwkv_r_k=rwkv_r_k,
             rwkv_ln_w=rwkv_ln_w, rwkv_ln_b=rwkv_ln_b, swa_sinks=swa_sinks,
             lru_conv_w=lru_conv_w, lru_conv_b=lru_conv_b, lru_wa=lru_wa, lru_ba=lru_ba,
             lru_wi=lru_wi, lru_bi=lru_bi, lru_lambda=lru_lambda, w_out=w_out,
             norm_mlp=norm_mlp, mlp_w1=mlp_w1, mlp_w2=mlp_w2, norm_final=norm_final)
    bp = x_prompt.shape[0]

    def empty(s):
        return jnp.zeros((DEPTH, bp) + s.shape[2:], s.dtype)

    y_prompt, sp = run_trunk(x_prompt, 0, empty(state_ssm), empty(state_ssm_conv), empty(state_rwkv),
                             empty(state_rwkv_shift), empty(cache_swa_k), empty(cache_swa_v),
                             empty(state_lru), empty(state_lru_conv), w)
    y_sample, ss = run_trunk(x_sample, PAST_LEN, state_ssm, state_ssm_conv, state_rwkv,
                             state_rwkv_shift, cache_swa_k, cache_swa_v, state_lru,
                             state_lru_conv, w)
    return (y_prompt, y_sample,
            sp['ssm_h'], ss['ssm_h'], sp['ssm_conv'], ss['ssm_conv'],
            sp['rwkv_s'], ss['rwkv_s'], sp['rwkv_shift'], ss['rwkv_shift'],
            sp['swa_k'], ss['swa_k'], sp['swa_v'], ss['swa_v'],
            sp['lru_h'], ss['lru_h'], sp['lru_conv'], ss['lru_conv'])
```

```python
import functools
import math

import jax
import jax.numpy as jnp
from jax import lax
from jax.experimental import pallas as pl
from jax.experimental.pallas import tpu as pltpu

F32 = jnp.float32
BF16 = jnp.bfloat16

D_MODEL = 2048
DEPTH = 4
GW = 512
RMS_EPS = 1e-6
CONV_W = 4
SSM_HEADS = 8
SSM_STATE = 128
SSM_XBC = 1024
RWKV_COLS = 1792
RWKV_GN_EPS = 64e-5
SWA_HEADDIM = 64
WINDOW = 128
ROT_DIM = 16
ROPE_THETA = 500000.0
LRU_C = 8.0
D_FF = 4 * D_MODEL
PAST_LEN = 16384

PW = 5376
CB_XBC, CB_LRU = 0, 1
CB_Z, CB_R, CB_K, CB_V, CB_Q = 4, 5, 6, 7, 8
CB_KV, CB_LORA = 18, 19
CB_DT = 40

HIST = 8
VMEM_LIMIT = 56 * 1024 * 1024


def _cparams(sem):
    return pltpu.CompilerParams(dimension_semantics=sem, vmem_limit_bytes=VMEM_LIMIT)


def _mm(a, b):
    return jnp.dot(a.astype(BF16), b.astype(BF16), preferred_element_type=F32)


def _mm_nt(a, b):
    return lax.dot_general(a.astype(BF16), b.astype(BF16), (((1,), (1,)), ((), ())),
                           preferred_element_type=F32)


def _mm_tn(a, b):
    return lax.dot_general(a.astype(BF16), b.astype(BF16), (((0,), (0,)), ((), ())),
                           preferred_element_type=F32)


def _split3(x):
    x1 = x.astype(BF16)
    r = x - x1.astype(F32)
    x2 = r.astype(BF16)
    r = r - x2.astype(F32)
    return x1, x2, r.astype(BF16)


def _mm_exact_rhs(a, sel):
    p1, p2, p3 = _split3(a)
    return _mm(p1, sel) + _mm(p2, sel) + _mm(p3, sel)


def _mm_exact_lhs(sel, b):
    p1, p2, p3 = _split3(b)
    return _mm(sel, p1) + _mm(sel, p2) + _mm(sel, p3)


def _mm_nt_exact_lhs(sel, b):
    p1, p2, p3 = _split3(b)
    return _mm_nt(sel, p1) + _mm_nt(sel, p2) + _mm_nt(sel, p3)


def _softplus(x):
    return jnp.maximum(x, 0.0) + jnp.log1p(jnp.exp(-jnp.abs(x)))


def _sigmoid(x):
    return 1.0 / (1.0 + jnp.exp(-x))


def _iota(shape, dim):
    return lax.broadcasted_iota(jnp.int32, shape, dim)


def _ind(mask):
    return jnp.where(mask, 1.0, 0.0)


def _rms(x, g):
    ms = jnp.mean(x * x, axis=-1, keepdims=True)
    return x * lax.rsqrt(ms + RMS_EPS) * g


def _conv_from_hist(ext_ref, rows, cw_ref, cb_ref):
    acc = cb_ref[...] + cw_ref[0:1, :] * ext_ref[HIST - 3:HIST - 3 + rows, :]
    for j in range(1, CONV_W):
        acc = acc + cw_ref[j:j + 1, :] * ext_ref[HIST - 3 + j:HIST - 3 + j + rows, :]
    return acc


def _rmsnorm_kernel(x_ref, g_ref, o_ref):
    o_ref[...] = _rms(x_ref[...], g_ref[...]).astype(o_ref.dtype)


def _inproj_kernel(h_ref, w_ref, o_ref):
    o_ref[...] = jnp.dot(h_ref[...], w_ref[...], preferred_element_type=F32)


def _outproj_kernel(y0_ref, y1_ref, y2_ref, y3_ref, w_ref, x_ref, g_ref, x1_ref, h2_ref):
    acc = x_ref[...]
    for i, y_ref in enumerate((y0_ref, y1_ref, y2_ref, y3_ref)):
        acc = acc + jnp.dot(y_ref[...], w_ref[i * GW:(i + 1) * GW, :], preferred_element_type=F32)
    x1_ref[...] = acc
    h2_ref[...] = _rms(acc, g_ref[...]).astype(h2_ref.dtype)


def _mlp_kernel(h_ref, w1_ref, w2_ref, x_ref, g_ref, *out_refs, final):
    xo_ref = out_refs[0]
    k = pl.program_id(1)

    @pl.when(k == 0)
    def _():
        xo_ref[...] = x_ref[...]

    a = jnp.dot(h_ref[...], w1_ref[...], preferred_element_type=F32)
    a = jnp.square(jnp.maximum(a, 0.0)).astype(BF16)
    xo_ref[...] += jnp.dot(a, w2_ref[...], preferred_element_type=F32)

    @pl.when(k == pl.num_programs(1) - 1)
    def _():
        normed = _rms(xo_ref[...], g_ref[...])
        if final:
            xo_ref[...] = normed
        else:
            out_refs[1][...] = normed.astype(BF16)


def _ssm_kernel(*refs, Q, has_state):
    if has_state:
        (xbc_ref, z_ref, dt_ref, buf_ref, h0_ref, cw_ref, cb_ref, dtb_ref, alog_ref, de_ref, nw_ref,
         _, y_ref, hout_ref, ext_ref, h_ref) = refs
    else:
        (xbc_ref, z_ref, dt_ref, cw_ref, cb_ref, dtb_ref, alog_ref, de_ref, nw_ref,
         y_ref, hout_ref, ext_ref, h_ref) = refs
    c = pl.program_id(1)

    @pl.when(c == 0)
    def _():
        if has_state:
            ext_ref[0:HIST, :] = buf_ref[0]
            for g in range(2):
                h_ref[g] = h0_ref[0, 4 * g:4 * g + 4].reshape(256, SSM_STATE).T
        else:
            ext_ref[0:HIST, :] = jnp.zeros((HIST, SSM_XBC), F32)
            h_ref[...] = jnp.zeros(h_ref.shape, F32)

    ext_ref[HIST:HIST + Q, :] = xbc_ref[...]
    pre = _conv_from_hist(ext_ref, Q, cw_ref, cb_ref)
    ext_ref[0:HIST, :] = ext_ref[Q:Q + HIST, :]
    xbc = pre * _sigmoid(pre)
    x = xbc[:, 0:GW]
    bm = xbc[:, GW:GW + 256]
    cm = xbc[:, GW + 256:GW + 512]

    dt = _softplus(dt_ref[...] + dtb_ref[...])
    a_neg = jnp.where(_iota((1, 128), 1) < SSM_HEADS, -jnp.exp(alog_ref[...]), 0.0)
    tri = (_iota((Q, Q), 0) >= _iota((Q, Q), 1))
    cum = _mm_exact_lhs(_ind(tri).astype(BF16), dt * a_neg)
    expand = _ind(_iota((128, GW), 0) == (_iota((128, GW), 1) >> 6)).astype(BF16)
    dt_e = _mm_exact_rhs(dt, expand)
    cum_e = _mm_exact_rhs(cum, expand)
    sel = _ind(_iota((SSM_HEADS, 128), 0) == _iota((SSM_HEADS, 128), 1)).astype(BF16)
    cum_t = _mm_nt_exact_lhs(sel, cum)
    last = cum_e[Q - 1:Q, :]
    expc_e = jnp.exp(cum_e)
    tail_e = jnp.exp(last - cum_e)
    dtot_e = jnp.exp(last)
    dx = dt_e * x
    lanehead = _iota((1, 256), 1) >> 6

    ys = []
    for g in range(2):
        bg = bm[:, g * 128:(g + 1) * 128]
        cg = cm[:, g * 128:(g + 1) * 128]
        cb = _mm_nt(cg, bg)
        h_t = h_ref[g]
        yg = _mm(cg, h_t) * expc_e[:, g * 256:(g + 1) * 256]
        dxg = dx[:, g * 256:(g + 1) * 256]
        for hh in range(4):
            h = 4 * g + hh
            decay = jnp.exp(jnp.minimum(cum[:, h:h + 1] - cum_t[h:h + 1, :], 0.0))
            m = jnp.where(tri, cb * decay, 0.0)
            yg = yg + _mm(m, jnp.where(lanehead == hh, dxg, 0.0))
        ys.append(yg)
        h_ref[g] = (h_t * dtot_e[:, g * 256:(g + 1) * 256]
                    + _mm_tn(bg, dxg * tail_e[:, g * 256:(g + 1) * 256]))

    y = jnp.concatenate(ys, axis=1) + de_ref[...] * x
    z = z_ref[...]
    y = y * (z * _sigmoid(z))
    outs = []
    for g in range(2):
        yg = y[:, g * 256:(g + 1) * 256]
        outs.append(yg * lax.rsqrt(jnp.mean(yg * yg, axis=-1, keepdims=True) + RMS_EPS))
    y_ref[...] = (jnp.concatenate(outs, axis=1) * nw_ref[...]).astype(y_ref.dtype)

    @pl.when(c == pl.num_programs(1) - 1)
    def _():
        for g in range(2):
            hout_ref[0, 4 * g:4 * g + 4] = h_ref[g].T.reshape(4, 64, SSM_STATE)


def _lru_kernel(*refs, R, has_state):
    if has_state:
        (u_ref, buf_ref, h0_ref, cw_ref, cb_ref, wa_ref, ba_ref, wi_ref, bi_ref, lam_ref,
         _, y_ref, hout_ref, ext_ref, h_ref) = refs
    else:
        (u_ref, cw_ref, cb_ref, wa_ref, ba_ref, wi_ref, bi_ref, lam_ref,
         y_ref, hout_ref, ext_ref, h_ref) = refs
    c = pl.program_id(1)

    @pl.when(c == 0)
    def _():
        if has_state:
            ext_ref[0:HIST, :] = buf_ref[0]
            h_ref[...] = h0_ref[0]
        else:
            ext_ref[0:HIST, :] = jnp.zeros((HIST, GW), F32)
            h_ref[...] = jnp.zeros(h_ref.shape, F32)

    u = u_ref[...]
    gate = u[:, GW:]
    ext_ref[HIST:HIST + R, :] = u[:, :GW]
    xc = _conv_from_hist(ext_ref, R, cw_ref, cb_ref)
    ext_ref[0:HIST, :] = ext_ref[R:R + HIST, :]

    rg = _sigmoid(_mm(xc, wa_ref[...]) + ba_ref[...])
    ig = _sigmoid(_mm(xc, wi_ref[...]) + bi_ref[...])
    log_a = -LRU_C * rg * _softplus(-lam_ref[...])
    a = jnp.exp(log_a)
    th = jnp.tanh(log_a)
    b = jnp.sqrt(-2.0 * th / (1.0 - th)) * (ig * xc)

    row = _iota((R, 1), 0)
    s = 1
    while s < R:
        keep = row >= s
        a_prev = jnp.where(keep, pltpu.roll(a, s, 0), 1.0)
        b_prev = jnp.where(keep, pltpu.roll(b, s, 0), 0.0)
        b = b + a * b_prev
        a = a * a_prev
        s *= 2
    h = b + a * h_ref[...]
    h_ref[...] = h[R - 1:R, :]
    hout_ref[0] = h[R - 1:R, :]

    gelu = 0.5 * gate * (1.0 + jnp.tanh(math.sqrt(2.0 / math.pi) * (gate + 0.044715 * (gate * gate * gate))))
    y_ref[...] = (h * gelu).astype(y_ref.dtype)


def _swa_kernel(*refs, QB, has_state):
    if has_state:
        (q_ref, kv_ref, kc_ref, vc_ref, cos_ref, sa_ref, sb_ref, sink_ref,
         _, y_ref, kout_ref, vout_ref, hk_ref, hv_ref) = refs
    else:
        (q_ref, kv_ref, cos_ref, sa_ref, sb_ref, sink_ref,
         y_ref, kout_ref, vout_ref, hk_ref, hv_ref) = refs
    c = pl.program_id(1)
    nk = WINDOW + QB

    @pl.when(c == 0)
    def _():
        if has_state:
            hk_ref[...] = kc_ref[0]
            hv_ref[...] = vc_ref[0]
        else:
            hk_ref[...] = jnp.zeros(hk_ref.shape, F32)
            hv_ref[...] = jnp.zeros(hv_ref.shape, F32)

    def rope(x, width):
        cos, sa, sb = cos_ref[:, 0:width], sa_ref[:, 0:width], sb_ref[:, 0:width]
        return x * cos + pltpu.roll(x, width - ROT_DIM // 2, 1) * sa + pltpu.roll(x, ROT_DIM // 2, 1) * sb

    q = rope(q_ref[...], GW) * (SWA_HEADDIM ** -0.5)
    kv = kv_ref[...]
    k_new = rope(kv[:, 0:128], 128)
    v_new = kv[:, 128:256]
    keys = jnp.concatenate([hk_ref[...], k_new], axis=0)
    vals = jnp.concatenate([hv_ref[...], v_new], axis=0)

    qi = _iota((QB, nk), 0)
    kj = _iota((QB, nk), 1)
    lo = qi if has_state else jnp.where(c == 0, jnp.maximum(qi, WINDOW), qi)
    neg_mask = jnp.where(kj >= lo, jnp.where(kj <= qi + WINDOW, 0.0, -jnp.inf), -jnp.inf)
    upper = _iota((1, 128), 1) >= SWA_HEADDIM
    lower = _iota((1, 128), 1) < SWA_HEADDIM

    tiles = [None] * 4
    for g in range(2):
        own = upper if g == 1 else lower
        kd = jnp.where(own, keys, pltpu.roll(keys, SWA_HEADDIM, 1))
        vd = jnp.where(own, vals, pltpu.roll(vals, SWA_HEADDIM, 1))
        for hh in range(4):
            h = 4 * g + hh
            mine = upper if h % 2 == 1 else lower
            qt = jnp.where(mine, q[:, 128 * (h // 2):128 * (h // 2 + 1)], 0.0)
            s = _mm_nt(qt, kd) + neg_mask
            sink = sink_ref[h:h + 1, 0:1]
            m = jnp.maximum(jnp.max(s, axis=-1, keepdims=True), sink)
            e = jnp.exp(s - m)
            den = jnp.sum(e, axis=-1, keepdims=True) + jnp.exp(sink - m)
            o = _mm(e / den, vd)
            tiles[h // 2] = o if h % 2 == 0 else jnp.where(mine, o, tiles[h // 2])
    y_ref[...] = jnp.concatenate(tiles, axis=1).astype(y_ref.dtype)

    if QB == WINDOW:
        hk_ref[...] = k_new
        hv_ref[...] = v_new
    else:
        hk_ref[...] = keys[QB:, :]
        hv_ref[...] = vals[QB:, :]
    kout_ref[0] = hk_ref[...]
    vout_ref[0] = hv_ref[...]


def _seg_sum(x):
    ones = _ind((_iota((256, 256), 0) >> 6) == (_iota((256, 256), 1) >> 6)).astype(BF16)
    return jnp.concatenate([_mm_exact_rhs(x[:, 0:256], ones), _mm_exact_rhs(x[:, 256:512], ones)], axis=1)


def _rwkv_kernel(*refs, G, C, has_state):
    if has_state:
        (r_ref, k_ref, v_ref, lo_ref, sh_ref, h0_ref, mu_ref, w0_ref, w2_ref, a0_ref, a2_ref, g2_ref,
         kk_ref, ka_ref, rk_ref, lnw_ref, lnb_ref, _, y_ref, hout_ref, hs_ref, carry_ref) = refs
    else:
        (r_ref, k_ref, v_ref, lo_ref, mu_ref, w0_ref, w2_ref, a0_ref, a2_ref, g2_ref,
         kk_ref, ka_ref, rk_ref, lnw_ref, lnb_ref, y_ref, hout_ref, hs_ref, carry_ref) = refs
    c = pl.program_id(1)
    R = G * C
    lc = C.bit_length() - 1

    @pl.when(c == 0)
    def _():
        if has_state:
            hs_ref[...] = h0_ref[...]
        else:
            hs_ref[...] = jnp.zeros(hs_ref.shape, F32)
            carry_ref[...] = jnp.zeros(carry_ref.shape, F32)

    first = (_iota((R, 1), 0) & (C - 1)) == 0
    mixed = []
    for ref, off, width in ((r_ref, 0, GW), (k_ref, GW, GW), (v_ref, 2 * GW, GW), (lo_ref, 3 * GW, 256)):
        pf = ref[...]
        if has_state:
            fv = sh_ref[:, off:off + width]
        else:
            fv = carry_ref[0:1, off:off + width]
        shifted = jnp.where(first, fv, pltpu.roll(pf, 1, 0))
        mixed.append(pf + (shifted - pf) * mu_ref[:, off:off + width])
        if not has_state:
            carry_ref[0:1, off:off + width] = pf[R - 1:R, :]
    r, k, v, lo = mixed

    lo_wa = lo[:, 0:128]
    w_log = -_softplus(-(w0_ref[...] + _mm(jnp.tanh(lo_wa), w2_ref[...]))) - 0.5
    lw = -jnp.exp(w_log)
    a = _sigmoid(a0_ref[...] + _mm(lo_wa, a2_ref[...]))
    gate = _mm(_sigmoid(lo[:, 128:256]), g2_ref[...])
    kk = k * kk_ref[...]
    kmod = k * (1.0 + (a - 1.0) * ka_ref[...])
    kk = kk * lax.rsqrt(jnp.maximum(_seg_sum(kk * kk), 1e-24))
    kb = kk * a

    NB = 4 * R
    lanehead = _iota((1, 256), 1) >> 6
    bi, bj = _iota((NB, NB), 0), _iota((NB, NB), 1)
    same = _ind((bi >> lc) == (bj >> lc))
    strict = same * _ind(bj < bi)
    incl = same * _ind(bj <= bi)
    eye_nb = _ind(bi == bj)
    si, sj = _iota((R, R), 0), _iota((R, R), 1)
    seg = _ind((si >> lc) == (sj >> lc))
    seg_ones = seg.astype(BF16)
    seg_tri = (seg * _ind(sj <= si)).astype(BF16)
    eye256 = _iota((256, 256), 0) == _iota((256, 256), 1)

    def bd(x):
        parts = []
        for s in range(G):
            xs = x[s * C:(s + 1) * C]
            for h in range(4):
                parts.append(jnp.where(lanehead == h, xs, 0.0))
        return jnp.concatenate(parts, axis=0)

    halves = []
    for hh in range(2):
        sl = slice(hh * 256, (hh + 1) * 256)
        lw4 = lw[:, sl]
        clw = _mm_exact_lhs(seg_tri, lw4)
        clt = _mm_exact_lhs(seg_ones, lw4)
        e_in = jnp.exp(clw)
        e_out = jnp.exp(-clw)
        e_end = jnp.exp(clt - clw)
        b_kq = bd(kk[:, sl] * jnp.exp(clw - lw4)).astype(BF16)
        b_rq = bd(r[:, sl] * e_in)
        b_kk = bd(kmod[:, sl] * e_out).astype(BF16)
        b_bk = bd(kb[:, sl] * e_out).astype(BF16)
        b_kk2 = bd(kmod[:, sl] * e_end).astype(BF16)
        b_bk2 = bd(kb[:, sl] * e_end).astype(BF16)
        b_v = bd(v[:, sl]).astype(BF16)
        p_end = jnp.exp(clt)
        b_rq16 = b_rq.astype(BF16)

        n_mat = strict * _mm_nt(b_kq, b_bk)
        a_kk = strict * _mm_nt(b_kq, b_kk)
        a_rb = incl * _mm_nt(b_rq16, b_bk)
        a_rk = incl * _mm_nt(b_rq16, b_kk)

        t_inv = eye_nb - n_mat
        pw = n_mat
        for _ in range(lc - 1):
            pw = _mm(pw, pw)
            t_inv = t_inv + _mm(t_inv, pw)

        av = _mm(a_kk, b_v)
        wu = _mm(t_inv, jnp.concatenate([b_kq, av.astype(BF16)], axis=1))
        arb_wu = _mm(a_rb, wu)
        rq2 = b_rq - arb_wu[:, 0:256]
        y_loc = _mm(a_rk, b_v) - arb_wu[:, 256:512]

        ys = []
        for s in range(G):
            rs = slice(s * 4 * C, (s + 1) * 4 * C)
            h0 = hs_ref[s, hh]
            h0b = h0.astype(BF16)
            yb = _mm(rq2[rs], h0b) + y_loc[rs]
            y4 = yb[0:C]
            for h in range(1, 4):
                y4 = y4 + yb[h * C:(h + 1) * C]
            ys.append(y4)
            bw = _mm_tn(b_bk2[rs], wu[rs])
            kv = _mm_tn(b_kk2[rs], b_v[rs])
            p_col = jnp.sum(jnp.where(eye256, p_end[s * C:s * C + 1, :], 0.0), axis=1, keepdims=True)
            hs_ref[s, hh] = p_col * h0 - _mm(bw[:, 0:256], h0b) + (kv - bw[:, 256:512])
        halves.append(ys[0] if G == 1 else jnp.concatenate(ys, axis=0))

    y = jnp.concatenate(halves, axis=1)
    mean = _seg_sum(y) * (1.0 / 64.0)
    yc = y - mean
    var = _seg_sum(yc * yc) * (1.0 / 64.0)
    yn = yc * lax.rsqrt(var + RWKV_GN_EPS) * lnw_ref[...] + lnb_ref[...]
    bonus = _seg_sum(r * kmod * rk_ref[...]) * v
    y_ref[...] = ((yn + bonus) * gate).astype(y_ref.dtype)

    @pl.when(c == pl.num_programs(1) - 1)
    def _():
        hout_ref[...] = hs_ref[...]


def _row(n):
    return pl.BlockSpec((1, n), lambda *_: (0, 0))


def _full(shape):
    nd = len(shape)
    return pl.BlockSpec(shape, lambda *_: (0,) * nd)


def _any():
    return pl.BlockSpec(memory_space=pl.ANY)


def _rmsnorm(x, g, tm):
    t = x.shape[0]
    return pl.pallas_call(
        _rmsnorm_kernel,
        grid=(t // tm,),
        in_specs=[pl.BlockSpec((tm, D_MODEL), lambda i: (i, 0)), _row(D_MODEL)],
        out_specs=pl.BlockSpec((tm, D_MODEL), lambda i: (i, 0)),
        out_shape=jax.ShapeDtypeStruct((t, D_MODEL), BF16),
        compiler_params=_cparams(("parallel",)),
        name="rmsnorm",
    )(x, g)


def _inproj(h, w, tm, tn):
    t = h.shape[0]
    return pl.pallas_call(
        _inproj_kernel,
        grid=(t // tm, PW // tn),
        in_specs=[pl.BlockSpec((tm, D_MODEL), lambda i, j: (i, 0)),
                  pl.BlockSpec((D_MODEL, tn), lambda i, j: (0, j))],
        out_specs=pl.BlockSpec((tm, tn), lambda i, j: (i, j)),
        out_shape=jax.ShapeDtypeStruct((t, PW), F32),
        compiler_params=_cparams(("parallel", "parallel")),
        name="inproj",
    )(h, w)


def _outproj(ys, w, x, g, tm):
    t = x.shape[0]
    yspec = pl.BlockSpec((tm, GW), lambda i: (i, 0))
    xspec = pl.BlockSpec((tm, D_MODEL), lambda i: (i, 0))
    return pl.pallas_call(
        _outproj_kernel,
        grid=(t // tm,),
        in_specs=[yspec, yspec, yspec, yspec, _full((D_MODEL, D_MODEL)), xspec, _row(D_MODEL)],
        out_specs=[xspec, xspec],
        out_shape=[jax.ShapeDtypeStruct((t, D_MODEL), F32), jax.ShapeDtypeStruct((t, D_MODEL), BF16)],
        compiler_params=_cparams(("parallel",)),
        name="outproj",
    )(*ys, w, x, g)


def _mlp(h, w1, w2, x, g, tm, tf, final):
    t = x.shape[0]
    xspec = pl.BlockSpec((tm, D_MODEL), lambda i, k: (i, 0))
    out_specs = [xspec] if final else [xspec, xspec]
    out_shape = [jax.ShapeDtypeStruct((t, D_MODEL), F32)]
    if not final:
        out_shape.append(jax.ShapeDtypeStruct((t, D_MODEL), BF16))
    return pl.pallas_call(
        functools.partial(_mlp_kernel, final=final),
        grid=(t // tm, D_FF // tf),
        in_specs=[xspec,
                  pl.BlockSpec((D_MODEL, tf), lambda i, k: (0, k)),
                  pl.BlockSpec((tf, D_MODEL), lambda i, k: (k, 0)),
                  xspec, _row(D_MODEL)],
        out_specs=out_specs,
        out_shape=out_shape,
        compiler_params=_cparams(("parallel", "arbitrary")),
        name="mlp",
    )(h, w1, w2, x, g)


def _mixer_call(kernel_fn, name, p, t_rows, n_seq, n_blk, rows, row0, col_specs, state_in, params, y_prev,
                state_out, scratch):
    def rowmap(cb):
        return lambda s, c: (row0 + s * n_blk + c, cb)

    in_specs = [pl.BlockSpec((rows, w), rowmap(cb)) for w, cb in col_specs]
    args = [p] * len(col_specs)
    for arr, blk in state_in:
        nd = len(blk)
        in_specs.append(pl.BlockSpec(blk, lambda s, c, nd=nd: (s,) + (0,) * (nd - 1)))
        args.append(arr)
    for arr in params:
        in_specs.append(_full(arr.shape))
        args.append(arr)
    aliases = {}
    if y_prev is not None:
        aliases = {len(args): 0}
        in_specs.append(_any())
        args.append(y_prev)
    out_specs = [pl.BlockSpec((rows, GW), lambda s, c: (row0 + s * n_blk + c, 0))]
    out_shape = [jax.ShapeDtypeStruct((t_rows, GW), BF16)]
    for shape, blk in state_out:
        nd = len(blk)
        out_specs.append(pl.BlockSpec(blk, lambda s, c, nd=nd: (s,) + (0,) * (nd - 1)))
        out_shape.append(jax.ShapeDtypeStruct(shape, F32))
    return pl.pallas_call(
        kernel_fn,
        grid=(n_seq, n_blk),
        in_specs=in_specs,
        out_specs=out_specs,
        out_shape=out_shape,
        scratch_shapes=scratch,
        input_output_aliases=aliases,
        compiler_params=_cparams(("arbitrary", "arbitrary")),
        name=name,
    )(*args)


def _rope_tables(pos):
    half = ROT_DIM // 2
    inv = ROPE_THETA ** (-(jnp.arange(half, dtype=F32) * 2.0) / ROT_DIM)
    ang = pos.astype(F32)[:, None] * inv
    cos, sin = jnp.cos(ang), jnp.sin(ang)
    n = pos.shape[0]
    zeros = jnp.zeros((n, half), F32)
    rest = SWA_HEADDIM - ROT_DIM
    cos64 = jnp.concatenate([cos, cos, jnp.ones((n, rest), F32)], axis=1)
    sa64 = jnp.concatenate([-sin, zeros, jnp.zeros((n, rest), F32)], axis=1)
    sb64 = jnp.concatenate([zeros, sin, jnp.zeros((n, rest), F32)], axis=1)
    return tuple(jnp.tile(t, (1, GW // SWA_HEADDIM)) for t in (cos64, sa64, sb64))


def _pad_rows_front(x, axis, total):
    pad = [(0, 0)] * x.ndim
    pad[axis] = (total - x.shape[axis], 0)
    return jnp.pad(x, pad)


def _rwkv_state_to_blockdiag(s):
    n = s.shape[0]
    st = jnp.swapaxes(s, -1, -2).reshape(n, 2, 4, 64, 64)
    eye = jnp.eye(4, dtype=s.dtype)
    return jnp.einsum('ngakv,ab->ngakbv', st, eye).reshape(n, 2, 256, 256)


def _rwkv_state_from_blockdiag(h):
    n = h.shape[0]
    h6 = h.reshape(n, 2, 4, 64, 4, 64)
    idx = jnp.arange(4)
    d = h6[:, :, idx, :, idx, :]
    return jnp.transpose(d, (1, 2, 0, 4, 3)).reshape(n, 8, 64, 64)


def kernel(x_prompt, x_sample, state_ssm, state_ssm_conv, state_rwkv, state_rwkv_shift, cache_swa_k,
           cache_swa_v, state_lru, state_lru_conv, norm_mix, w_in, ssm_conv_w, ssm_conv_b, ssm_dt_bias,
           ssm_a_log, ssm_d, ssm_norm, rwkv_mu, rwkv_w0, rwkv_w2, rwkv_a0, rwkv_a2, rwkv_g2, rwkv_k_k,
           rwkv_k_a, rwkv_r_k, rwkv_ln_w, rwkv_ln_b, swa_sinks, lru_conv_w, lru_conv_b, lru_wa, lru_ba,
           lru_wi, lru_bi, lru_lambda, w_out, norm_mlp, mlp_w1, mlp_w2, norm_final):
    bp, lp, _ = x_prompt.shape
    bs, ls, _ = x_sample.shape
    tp, ts = bp * lp, bs * ls
    t = tp + ts
    depth = w_in.shape[0]

    o_rwkv = GW + SSM_XBC + SSM_HEADS
    o_swa = o_rwkv + RWKV_COLS
    o_lru = o_swa + GW + 256
    w_in_r = jnp.concatenate([
        w_in[..., GW:GW + SSM_XBC],
        w_in[..., o_lru:o_lru + 2 * GW],
        w_in[..., 0:GW],
        w_in[..., o_rwkv:o_rwkv + 3 * GW],
        w_in[..., o_swa:o_swa + GW],
        w_in[..., o_swa + GW:o_swa + GW + 256],
        w_in[..., o_rwkv + 3 * GW:o_rwkv + RWKV_COLS],
        w_in[..., GW + SSM_XBC:GW + SSM_XBC + SSM_HEADS],
        jnp.zeros(w_in.shape[:2] + (PW - (o_lru + 2 * GW),), w_in.dtype),
    ], axis=-1).astype(BF16)
    w_out_b = w_out.astype(BF16)
    w1_b = mlp_w1.astype(BF16)
    w2_b = mlp_w2.astype(BF16)
    eye8 = jnp.eye(8, dtype=F32)
    lru_wa_d = jnp.einsum('lncd,nm->lncmd', lru_wa, eye8).reshape(depth, GW, GW).astype(BF16)
    lru_wi_d = jnp.einsum('lncd,nm->lncmd', lru_wi, eye8).reshape(depth, GW, GW).astype(BF16)
    z64 = jnp.zeros((depth, 64, GW), F32)
    rwkv_w2_p = jnp.concatenate([rwkv_w2, z64], axis=1).astype(BF16)
    rwkv_a2_p = jnp.concatenate([z64, rwkv_a2], axis=1).astype(BF16)
    rwkv_g2_b = rwkv_g2.astype(BF16)
    dtb_p = jnp.pad(ssm_dt_bias, ((0, 0), (0, 128 - SSM_HEADS)))[:, None, :]
    alog_p = jnp.pad(ssm_a_log, ((0, 0), (0, 128 - SSM_HEADS)))[:, None, :]
    d_e = jnp.repeat(ssm_d, 64, axis=-1)[:, None, :]
    sinks_b = jnp.broadcast_to(swa_sinks[:, :, None], (depth, 8, 128))
    rope_p = _rope_tables(jnp.arange(lp))
    rope_s = _rope_tables(PAST_LEN + jnp.arange(ls))

    def r2(a):
        return a[:, None, :]

    ssm_buf = _pad_rows_front(state_ssm_conv, 2, HIST)
    lru_buf = _pad_rows_front(state_lru_conv, 2, HIST)
    lru_h0 = state_lru[:, :, None, :]
    sh_rows = jnp.pad(state_rwkv_shift[:, :, None, :], ((0, 0), (0, 0), (0, ls - 1), (0, 0))).reshape(depth, ts, RWKV_COLS)
    swa_k0 = cache_swa_k.reshape(depth, bs, WINDOW, 128)
    swa_v0 = cache_swa_v.reshape(depth, bs, WINDOW, 128)

    x = jnp.concatenate([x_prompt.reshape(tp, D_MODEL), x_sample.reshape(ts, D_MODEL)], axis=0)
    h = _rmsnorm(x, norm_mix[0][None, :], 704)

    q_ssm, q_swa, r_lru, c_rwkv, g_rwkv = 128, 128, 256, 64, 4
    outs = {n: [] for n in ('ssm_p', 'ssm_s', 'conv_p', 'conv_s', 'rwkv_p', 'rwkv_s', 'shift_p', 'shift_s',
                            'k_p', 'k_s', 'v_p', 'v_s', 'lru_p', 'lru_s', 'lconv_p', 'lconv_s')}
    y_final = None
    for l in range(depth):
        p = _inproj(h, w_in_r[l], 1408, 768)

        ssm_params = [ssm_conv_w[l], ssm_conv_b[l][None, :], dtb_p[l], alog_p[l], d_e[l], ssm_norm[l][None, :]]
        ssm_cols = [(SSM_XBC, CB_XBC), (GW, CB_Z), (128, CB_DT)]
        y_ssm, hs_p = _mixer_call(
            functools.partial(_ssm_kernel, Q=q_ssm, has_state=False), "ssm_prompt", p, t, bp, lp // q_ssm, q_ssm, 0,
            ssm_cols, [], ssm_params, None,
            [((bp, SSM_HEADS, 64, SSM_STATE), (1, SSM_HEADS, 64, SSM_STATE))],
            [pltpu.VMEM((HIST + q_ssm, SSM_XBC), F32), pltpu.VMEM((2, SSM_STATE, 256), F32)])
        y_ssm, hs_s = _mixer_call(
            functools.partial(_ssm_kernel, Q=ls, has_state=True), "ssm_sample", p, t, bs, 1, ls, tp // ls,
            ssm_cols, [(ssm_buf[l], (1, HIST, SSM_XBC)), (state_ssm[l], (1, SSM_HEADS, 64, SSM_STATE))],
            ssm_params, y_ssm,
            [((bs, SSM_HEADS, 64, SSM_STATE), (1, SSM_HEADS, 64, SSM_STATE))],
            [pltpu.VMEM((HIST + ls, SSM_XBC), F32), pltpu.VMEM((2, SSM_STATE, 256), F32)])

        rwkv_params = [r2(rwkv_mu)[l], r2(rwkv_w0)[l], rwkv_w2_p[l], r2(rwkv_a0)[l], rwkv_a2_p[l], rwkv_g2_b[l],
                       r2(rwkv_k_k)[l], r2(rwkv_k_a)[l], rwkv_r_k[l].reshape(1, GW), r2(rwkv_ln_w)[l],
                       r2(rwkv_ln_b)[l]]
        rwkv_cols = [(GW, CB_R), (GW, CB_K), (GW, CB_V), (256, CB_LORA)]
        y_rwkv, s_p = _mixer_call(
            functools.partial(_rwkv_kernel, G=1, C=c_rwkv, has_state=False), "rwkv_prompt", p, t, bp,
            lp // c_rwkv, c_rwkv, 0, rwkv_cols, [], rwkv_params, None,
            [((bp, 2, 256, 256), (1, 2, 256, 256))],
            [pltpu.VMEM((1, 2, 256, 256), F32), pltpu.VMEM((HIST, RWKV_COLS), F32)])
        rows_s = g_rwkv * ls
        y_rwkv, s_s = _mixer_call(
            functools.partial(_rwkv_kernel, G=g_rwkv, C=ls, has_state=True), "rwkv_sample", p, t, bs // g_rwkv, 1,
            rows_s, tp // rows_s, rwkv_cols,
            [(sh_rows[l], (rows_s, RWKV_COLS)), (_rwkv_state_to_blockdiag(state_rwkv[l]), (g_rwkv, 2, 256, 256))],
            rwkv_params, y_rwkv,
            [((bs, 2, 256, 256), (g_rwkv, 2, 256, 256))],
            [pltpu.VMEM((g_rwkv, 2, 256, 256), F32), pltpu.VMEM((HIST, RWKV_COLS), F32)])

        swa_cols = [(GW, CB_Q), (256, CB_KV)]
        swa_state_out = lambda n: [((n, WINDOW, 128), (1, WINDOW, 128)), ((n, WINDOW, 128), (1, WINDOW, 128))]
        swa_scratch = [pltpu.VMEM((WINDOW, 128), F32), pltpu.VMEM((WINDOW, 128), F32)]
        y_swa, k_p, v_p = _swa_call(p, t, bp, lp // q_swa, q_swa, 0, swa_cols, [], rope_p, sinks_b[l], None,
                                    swa_state_out(bp), swa_scratch, False, "swa_prompt")
        y_swa, k_s, v_s = _swa_call(p, t, bs, 1, ls, tp // ls, swa_cols,
                                    [(swa_k0[l], (1, WINDOW, 128)), (swa_v0[l], (1, WINDOW, 128))],
                                    rope_s, sinks_b[l], y_swa, swa_state_out(bs), swa_scratch, True, "swa_sample")

        lru_params = [lru_conv_w[l], lru_conv_b[l][None, :], lru_wa_d[l], lru_ba[l][None, :], lru_wi_d[l],
                      lru_bi[l][None, :], lru_lambda[l][None, :]]
        lru_cols = [(2 * GW, CB_LRU)]
        y_lru, lh_p = _mixer_call(
            functools.partial(_lru_kernel, R=r_lru, has_state=False), "lru_prompt", p, t, bp, lp // r_lru, r_lru, 0,
            lru_cols, [], lru_params, None, [((bp, 1, GW), (1, 1, GW))],
            [pltpu.VMEM((HIST + r_lru, GW), F32), pltpu.VMEM((1, GW), F32)])
        y_lru, lh_s = _mixer_call(
            functools.partial(_lru_kernel, R=ls, has_state=True), "lru_sample", p, t, bs, 1, ls, tp // ls,
            lru_cols, [(lru_buf[l], (1, HIST, GW)), (lru_h0[l], (1, 1, GW))], lru_params, y_lru,
            [((bs, 1, GW), (1, 1, GW))],
            [pltpu.VMEM((HIST + ls, GW), F32), pltpu.VMEM((1, GW), F32)])

        x1, h2 = _outproj([y_ssm, y_rwkv, y_swa, y_lru], w_out_b[l], x, norm_mlp[l][None, :], 352)
        if l + 1 < depth:
            x, h = _mlp(h2, w1_b[l], w2_b[l], x1, norm_mix[l + 1][None, :], 704, 512, False)
        else:
            (y_final,) = _mlp(h2, w1_b[l], w2_b[l], x1, norm_final[None, :], 704, 512, True)

        pp = p[:tp].reshape(bp, lp, PW)
        ps = p[tp:].reshape(bs, ls, PW)
        rwkv_row = lambda a: jnp.concatenate([a[..., 5 * GW:8 * GW], a[..., CB_LORA * 256:CB_LORA * 256 + 256]], axis=-1)
        outs['ssm_p'].append(hs_p)
        outs['ssm_s'].append(hs_s)
        outs['conv_p'].append(pp[:, lp - 3:, 0:SSM_XBC])
        outs['conv_s'].append(ps[:, ls - 3:, 0:SSM_XBC])
        outs['rwkv_p'].append(_rwkv_state_from_blockdiag(s_p))
        outs['rwkv_s'].append(_rwkv_state_from_blockdiag(s_s))
        outs['shift_p'].append(rwkv_row(pp[:, lp - 1]))
        outs['shift_s'].append(rwkv_row(ps[:, ls - 1]))
        outs['k_p'].append(k_p.reshape(bp, WINDOW, 2, SWA_HEADDIM))
        outs['k_s'].append(k_s.reshape(bs, WINDOW, 2, SWA_HEADDIM))
        outs['v_p'].append(v_p.reshape(bp, WINDOW, 2, SWA_HEADDIM))
        outs['v_s'].append(v_s.reshape(bs, WINDOW, 2, SWA_HEADDIM))
        outs['lru_p'].append(lh_p[:, 0])
        outs['lru_s'].append(lh_s[:, 0])
        outs['lconv_p'].append(pp[:, lp - 3:, SSM_XBC:SSM_XBC + GW])
        outs['lconv_s'].append(ps[:, ls - 3:, SSM_XBC:SSM_XBC + GW])

    st = {n: jnp.stack(v, axis=0) for n, v in outs.items()}
    return (y_final[:tp].reshape(bp, lp, D_MODEL), y_final[tp:].reshape(bs, ls, D_MODEL),
            st['ssm_p'], st['ssm_s'], st['conv_p'], st['conv_s'], st['rwkv_p'], st['rwkv_s'],
            st['shift_p'], st['shift_s'], st['k_p'], st['k_s'], st['v_p'], st['v_s'],
            st['lru_p'], st['lru_s'], st['lconv_p'], st['lconv_s'])


def _swa_call(p, t, n_seq, n_blk, qb, row0, cols, state_in, rope, sinks, y_prev, state_out, scratch, has_state, name):
    def rowmap(cb):
        return lambda s, c: (row0 + s * n_blk + c, cb)

    in_specs = [pl.BlockSpec((qb, w), rowmap(cb)) for w, cb in cols]
    args = [p] * len(cols)
    for arr, blk in state_in:
        in_specs.append(pl.BlockSpec(blk, lambda s, c: (s, 0, 0)))
        args.append(arr)
    for tab in rope:
        in_specs.append(pl.BlockSpec((qb, GW), lambda s, c: (c, 0)))
        args.append(tab)
    in_specs.append(_full(sinks.shape))
    args.append(sinks)
    aliases = {}
    if y_prev is not None:
        aliases = {len(args): 0}
        in_specs.append(_any())
        args.append(y_prev)
    out_specs = [pl.BlockSpec((qb, GW), lambda s, c: (row0 + s * n_blk + c, 0))]
    out_shape = [jax.ShapeDtypeStruct((t, GW), BF16)]
    for shape, blk in state_out:
        out_specs.append(pl.BlockSpec(blk, lambda s, c: (s, 0, 0)))
        out_shape.append(jax.ShapeDtypeStruct(shape, F32))
    return pl.pallas_call(
        functools.partial(_swa_kernel, QB=qb, has_state=has_state),
        grid=(n_seq, n_blk),
        in_specs=in_specs,
        out_specs=out_specs,
        out_shape=out_shape,
        scratch_shapes=scratch,
        input_output_aliases=aliases,
        compiler_params=_cparams(("arbitrary", "arbitrary")),
        name=name,
    )(*args)
```

```python
import functools
import math

import jax
import jax.numpy as jnp
from jax import lax
from jax.experimental import pallas as pl
from jax.experimental.pallas import tpu as pltpu

F32 = jnp.float32
BF16 = jnp.bfloat16

D_MODEL = 2048
GW = 512
RMS_EPS = 1e-6
CONV_W = 4
SSM_HEADS = 8
SSM_STATE = 128
SSM_XBC = 1024
RWKV_COLS = 1792
RWKV_GN_EPS = 64e-5
SWA_HEADDIM = 64
WINDOW = 128
ROT_DIM = 16
ROPE_THETA = 500000.0
LRU_C = 8.0
D_FF = 4 * D_MODEL
PAST_LEN = 16384

PW = 5376
CB_XBC, CB_LRU = 0, 1
CB_Z, CB_R, CB_K, CB_V, CB_Q = 4, 5, 6, 7, 8
CB_KV, CB_LORA = 18, 19
CB_DT = 40

HIST = 8
VMEM_LIMIT = 56 * 1024 * 1024

TILES = dict(
    norm_tm=704, in_tm=1408, in_tn=768, out_tm=352, mlp_tm=704, mlp_tf=512,
    ssm_q=128, swa_q=128, lru_r=256, rwkv_c=64, rwkv_sub=2,
    swa_ns=4, rwkv_g=4, rwkv_sub_s=2,
)


def _cparams(sem):
    return pltpu.CompilerParams(dimension_semantics=sem, vmem_limit_bytes=VMEM_LIMIT)


def _mm(a, b):
    return jnp.dot(a.astype(BF16), b.astype(BF16), preferred_element_type=F32)


def _mm_nt(a, b):
    return lax.dot_general(a.astype(BF16), b.astype(BF16), (((1,), (1,)), ((), ())),
                           preferred_element_type=F32)


def _mm_tn(a, b):
    return lax.dot_general(a.astype(BF16), b.astype(BF16), (((0,), (0,)), ((), ())),
                           preferred_element_type=F32)


def _split3(x):
    x1 = x.astype(BF16)
    r = x - x1.astype(F32)
    x2 = r.astype(BF16)
    r = r - x2.astype(F32)
    return x1, x2, r.astype(BF16)


def _mm_exact_rhs(a, sel):
    p1, p2, p3 = _split3(a)
    return _mm(p1, sel) + _mm(p2, sel) + _mm(p3, sel)


def _mm_exact_lhs(sel, b):
    p1, p2, p3 = _split3(b)
    return _mm(sel, p1) + _mm(sel, p2) + _mm(sel, p3)


def _mm_nt_exact_lhs(sel, b):
    p1, p2, p3 = _split3(b)
    return _mm_nt(sel, p1) + _mm_nt(sel, p2) + _mm_nt(sel, p3)


def _softplus(x):
    return jnp.maximum(x, 0.0) + jnp.log1p(jnp.exp(-jnp.abs(x)))


def _sigmoid(x):
    return 1.0 / (1.0 + jnp.exp(-x))


def _iota(shape, dim):
    return lax.broadcasted_iota(jnp.int32, shape, dim)


def _ind(mask):
    return jnp.where(mask, 1.0, 0.0)


def _rms(x, g):
    ms = jnp.mean(x * x, axis=-1, keepdims=True)
    return x * lax.rsqrt(ms + RMS_EPS) * g


def _conv_from_hist(ext_ref, rows, cw_ref, cb_ref):
    acc = cb_ref[...] + cw_ref[0:1, :] * ext_ref[HIST - 3:HIST - 3 + rows, :]
    for j in range(1, CONV_W):
        acc = acc + cw_ref[j:j + 1, :] * ext_ref[HIST - 3 + j:HIST - 3 + j + rows, :]
    return acc


def _rmsnorm_kernel(x_ref, g_ref, o_ref):
    o_ref[...] = _rms(x_ref[...], g_ref[...]).astype(o_ref.dtype)


def _inproj_kernel(h_ref, w_ref, o_ref):
    o_ref[...] = jnp.dot(h_ref[...], w_ref[...], preferred_element_type=F32)


def _outproj_kernel(y0_ref, y1_ref, y2_ref, y3_ref, w_ref, x_ref, g_ref, x1_ref, h2_ref):
    acc = x_ref[...]
    for i, y_ref in enumerate((y0_ref, y1_ref, y2_ref, y3_ref)):
        acc = acc + jnp.dot(y_ref[...], w_ref[i * GW:(i + 1) * GW, :], preferred_element_type=F32)
    x1_ref[...] = acc
    h2_ref[...] = _rms(acc, g_ref[...]).astype(h2_ref.dtype)


def _mlp_kernel(h_ref, w1_ref, w2_ref, x_ref, g_ref, *out_refs, final):
    xo_ref = out_refs[0]
    k = pl.program_id(1)

    @pl.when(k == 0)
    def _():
        xo_ref[...] = x_ref[...]

    a = jnp.dot(h_ref[...], w1_ref[...], preferred_element_type=F32)
    a = jnp.square(jnp.maximum(a, 0.0)).astype(BF16)
    xo_ref[...] += jnp.dot(a, w2_ref[...], preferred_element_type=F32)

    @pl.when(k == pl.num_programs(1) - 1)
    def _():
        normed = _rms(xo_ref[...], g_ref[...])
        if final:
            xo_ref[...] = normed
        else:
            out_refs[1][...] = normed.astype(BF16)


def _ssm_kernel(*refs, Q, has_state):
    if has_state:
        (xbc_ref, z_ref, dt_ref, buf_ref, h0_ref, cw_ref, cb_ref, dtb_ref, alog_ref, de_ref, nw_ref,
         _, y_ref, hout_ref, ext_ref, h_ref) = refs
    else:
        (xbc_ref, z_ref, dt_ref, cw_ref, cb_ref, dtb_ref, alog_ref, de_ref, nw_ref,
         y_ref, hout_ref, ext_ref, h_ref) = refs
    c = pl.program_id(1)

    @pl.when(c == 0)
    def _():
        if has_state:
            ext_ref[0:HIST, :] = buf_ref[0]
            for g in range(2):
                h_ref[g] = h0_ref[0, 4 * g:4 * g + 4].reshape(256, SSM_STATE).T
        else:
            ext_ref[0:HIST, :] = jnp.zeros((HIST, SSM_XBC), F32)
            h_ref[...] = jnp.zeros(h_ref.shape, F32)

    ext_ref[HIST:HIST + Q, :] = xbc_ref[...]
    pre = _conv_from_hist(ext_ref, Q, cw_ref, cb_ref)
    ext_ref[0:HIST, :] = ext_ref[Q:Q + HIST, :]
    xbc = pre * _sigmoid(pre)
    x = xbc[:, 0:GW]
    bm = xbc[:, GW:GW + 256]
    cm = xbc[:, GW + 256:GW + 512]

    dt = _softplus(dt_ref[...] + dtb_ref[...])
    a_neg = jnp.where(_iota((1, 128), 1) < SSM_HEADS, -jnp.exp(alog_ref[...]), 0.0)
    tri = (_iota((Q, Q), 0) >= _iota((Q, Q), 1))
    cum = _mm_exact_lhs(_ind(tri).astype(BF16), dt * a_neg)
    expand = _ind(_iota((128, GW), 0) == (_iota((128, GW), 1) >> 6)).astype(BF16)
    dt_e = _mm_exact_rhs(dt, expand)
    cum_e = _mm_exact_rhs(cum, expand)
    sel = _ind(_iota((SSM_HEADS, 128), 0) == _iota((SSM_HEADS, 128), 1)).astype(BF16)
    cum_t = _mm_nt_exact_lhs(sel, cum)
    last = cum_e[Q - 1:Q, :]
    expc_e = jnp.exp(cum_e)
    tail_e = jnp.exp(last - cum_e)
    dtot_e = jnp.exp(last)
    dx = dt_e * x
    lanehead = _iota((1, 256), 1) >> 6

    ys = []
    for g in range(2):
        bg = bm[:, g * 128:(g + 1) * 128]
        cg = cm[:, g * 128:(g + 1) * 128]
        cb = _mm_nt(cg, bg)
        h_t = h_ref[g]
        yg = _mm(cg, h_t) * expc_e[:, g * 256:(g + 1) * 256]
        dxg = dx[:, g * 256:(g + 1) * 256]
        for hh in range(4):
            h = 4 * g + hh
            decay = jnp.exp(jnp.minimum(cum[:, h:h + 1] - cum_t[h:h + 1, :], 0.0))
            m = jnp.where(tri, cb * decay, 0.0)
            yg = yg + _mm(m, jnp.where(lanehead == hh, dxg, 0.0))
        ys.append(yg)
        h_ref[g] = (h_t * dtot_e[:, g * 256:(g + 1) * 256]
                    + _mm_tn(bg, dxg * tail_e[:, g * 256:(g + 1) * 256]))

    y = jnp.concatenate(ys, axis=1) + de_ref[...] * x
    z = z_ref[...]
    y = y * (z * _sigmoid(z))
    outs = []
    for g in range(2):
        yg = y[:, g * 256:(g + 1) * 256]
        outs.append(yg * lax.rsqrt(jnp.mean(yg * yg, axis=-1, keepdims=True) + RMS_EPS))
    y_ref[...] = (jnp.concatenate(outs, axis=1) * nw_ref[...]).astype(y_ref.dtype)

    @pl.when(c == pl.num_programs(1) - 1)
    def _():
        for g in range(2):
            hout_ref[0, 4 * g:4 * g + 4] = h_ref[g].T.reshape(4, 64, SSM_STATE)


def _lru_kernel(*refs, R, has_state):
    if has_state:
        (u_ref, buf_ref, h0_ref, cw_ref, cb_ref, wa_ref, ba_ref, wi_ref, bi_ref, lam_ref,
         _, y_ref, hout_ref, ext_ref, h_ref) = refs
    else:
        (u_ref, cw_ref, cb_ref, wa_ref, ba_ref, wi_ref, bi_ref, lam_ref,
         y_ref, hout_ref, ext_ref, h_ref) = refs
    c = pl.program_id(1)

    @pl.when(c == 0)
    def _():
        if has_state:
            ext_ref[0:HIST, :] = buf_ref[0]
            h_ref[...] = h0_ref[0]
        else:
            ext_ref[0:HIST, :] = jnp.zeros((HIST, GW), F32)
            h_ref[...] = jnp.zeros(h_ref.shape, F32)

    u = u_ref[...]
    gate = u[:, GW:]
    ext_ref[HIST:HIST + R, :] = u[:, :GW]
    xc = _conv_from_hist(ext_ref, R, cw_ref, cb_ref)
    ext_ref[0:HIST, :] = ext_ref[R:R + HIST, :]

    rg = _sigmoid(_mm(xc, wa_ref[...]) + ba_ref[...])
    ig = _sigmoid(_mm(xc, wi_ref[...]) + bi_ref[...])
    log_a = -LRU_C * rg * _softplus(-lam_ref[...])
    a = jnp.exp(log_a)
    th = jnp.tanh(log_a)
    b = jnp.sqrt(-2.0 * th / (1.0 - th)) * (ig * xc)

    row = _iota((R, 1), 0)
    s = 1
    while s < R:
        keep = row >= s
        a_prev = jnp.where(keep, pltpu.roll(a, s, 0), 1.0)
        b_prev = jnp.where(keep, pltpu.roll(b, s, 0), 0.0)
        b = b + a * b_prev
        a = a * a_prev
        s *= 2
    h = b + a * h_ref[...]
    h_ref[...] = h[R - 1:R, :]
    hout_ref[0] = h[R - 1:R, :]

    gelu = 0.5 * gate * (1.0 + jnp.tanh(math.sqrt(2.0 / math.pi) * (gate + 0.044715 * (gate * gate * gate))))
    y_ref[...] = (h * gelu).astype(y_ref.dtype)


def _swa_kernel(*refs, QB, NS, has_state):
    if has_state:
        (q_ref, kv_ref, kc_ref, vc_ref, cos_ref, sa_ref, sb_ref, sink_ref,
         _, y_ref, kout_ref, vout_ref, hk_ref, hv_ref) = refs
    else:
        (q_ref, kv_ref, cos_ref, sa_ref, sb_ref, sink_ref,
         y_ref, kout_ref, vout_ref, hk_ref, hv_ref) = refs
    c = pl.program_id(1)
    nk = WINDOW + QB
    lq = QB.bit_length() - 1

    @pl.when(c == 0)
    def _():
        if has_state:
            hk_ref[...] = kc_ref[...]
            hv_ref[...] = vc_ref[...]
        else:
            hk_ref[...] = jnp.zeros(hk_ref.shape, F32)
            hv_ref[...] = jnp.zeros(hv_ref.shape, F32)

    def rope(x, width):
        cos, sa, sb = cos_ref[:, 0:width], sa_ref[:, 0:width], sb_ref[:, 0:width]
        return x * cos + pltpu.roll(x, width - ROT_DIM // 2, 1) * sa + pltpu.roll(x, ROT_DIM // 2, 1) * sb

    kj = _iota((nk, 4 * QB), 0)
    qi = _iota((nk, 4 * QB), 1) & (QB - 1)
    lo = qi if has_state else jnp.where(c == 0, jnp.maximum(qi, WINDOW), qi)
    neg_mask = jnp.where(kj >= lo, jnp.where(kj <= qi + WINDOW, 0.0, -jnp.inf), -jnp.inf)
    upper = _iota((1, 128), 1) >= SWA_HEADDIM
    lower = _iota((1, 128), 1) < SWA_HEADDIM
    chead = _iota((1, 4 * QB), 1) >> lq
    sink_rows = []
    for g in range(2):
        sr = sink_ref[4 * g + 3:4 * g + 4, 0:1]
        for hh in (2, 1, 0):
            sr = jnp.where(chead == hh, sink_ref[4 * g + hh:4 * g + hh + 1, 0:1], sr)
        sink_rows.append(sr)

    for s in range(NS):
        rows = slice(s * QB, (s + 1) * QB)
        q = rope(q_ref[rows, :], GW) * (SWA_HEADDIM ** -0.5)
        kv = kv_ref[rows, :]
        k_new = rope(kv[:, 0:128], 128)
        v_new = kv[:, 128:256]
        keys = jnp.concatenate([hk_ref[s], k_new], axis=0)
        vals = jnp.concatenate([hv_ref[s], v_new], axis=0)

        tiles = [None] * 4
        for g in range(2):
            own = upper if g == 1 else lower
            kd = jnp.where(own, keys, pltpu.roll(keys, SWA_HEADDIM, 1))
            vd = jnp.where(own, vals, pltpu.roll(vals, SWA_HEADDIM, 1))
            parts = []
            for hh in range(4):
                h = 4 * g + hh
                mine = upper if h % 2 == 1 else lower
                parts.append(jnp.where(mine, q[:, 128 * (h // 2):128 * (h // 2 + 1)], 0.0))
            sc = _mm_nt(kd, jnp.concatenate(parts, axis=0)) + neg_mask
            m = jnp.maximum(jnp.max(sc, axis=0, keepdims=True), sink_rows[g])
            e = jnp.exp(sc - m)
            den = jnp.sum(e, axis=0, keepdims=True) + jnp.exp(sink_rows[g] - m)
            o = _mm_tn(e / den, vd)
            for hh in range(4):
                h = 4 * g + hh
                oh = o[hh * QB:(hh + 1) * QB]
                tiles[h // 2] = oh if h % 2 == 0 else jnp.where(upper, oh, tiles[h // 2])
        y_ref[rows, :] = jnp.concatenate(tiles, axis=1).astype(y_ref.dtype)

        if QB == WINDOW:
            hk_ref[s] = k_new
            hv_ref[s] = v_new
        else:
            hk_ref[s] = keys[QB:, :]
            hv_ref[s] = vals[QB:, :]
    kout_ref[...] = hk_ref[...]
    vout_ref[...] = hv_ref[...]


def _seg_sum(x):
    ones = _ind((_iota((256, 256), 0) >> 6) == (_iota((256, 256), 1) >> 6)).astype(BF16)
    return jnp.concatenate([_mm_exact_rhs(x[:, 0:256], ones), _mm_exact_rhs(x[:, 256:512], ones)], axis=1)


def _rwkv_kernel(*refs, NSUB, G, C, has_state):
    if has_state:
        (r_ref, k_ref, v_ref, lo_ref, sh_ref, h0_ref, mu_ref, w0_ref, w2_ref, a0_ref, a2_ref, g2_ref,
         kk_ref, ka_ref, rk_ref, lnw_ref, lnb_ref, _, y_ref, hout_ref, hs_ref, carry_ref) = refs
    else:
        (r_ref, k_ref, v_ref, lo_ref, mu_ref, w0_ref, w2_ref, a0_ref, a2_ref, g2_ref,
         kk_ref, ka_ref, rk_ref, lnw_ref, lnb_ref, y_ref, hout_ref, hs_ref, carry_ref) = refs
    c = pl.program_id(1)
    R = G * C
    RB = NSUB * R
    lc = C.bit_length() - 1
    n_slots = NSUB * G if has_state else 1
    blockmask = _ind((_iota((256, 256), 0) >> 6) == (_iota((256, 256), 1) >> 6))

    @pl.when(c == 0)
    def _():
        if has_state:
            unfold = _ind(_iota((64, 256), 0) == (_iota((64, 256), 1) & 63)).astype(BF16)
            for slot in range(n_slots):
                for hh in range(2):
                    s_flat = h0_ref[slot, 4 * hh:4 * hh + 4].reshape(256, 64)
                    hs_ref[slot, hh] = blockmask * _mm_exact_rhs(s_flat, unfold)
        else:
            hs_ref[...] = jnp.zeros(hs_ref.shape, F32)
            carry_ref[...] = jnp.zeros(carry_ref.shape, F32)

    row = _iota((RB, 1), 0)
    first = ((row & (C - 1)) == 0) if has_state else (row == 0)
    mixed = []
    for ref, off, width in ((r_ref, 0, GW), (k_ref, GW, GW), (v_ref, 2 * GW, GW), (lo_ref, 3 * GW, 256)):
        pf = ref[...]
        if has_state:
            fv = sh_ref[:, off:off + width]
        else:
            fv = carry_ref[0:1, off:off + width]
        shifted = jnp.where(first, fv, pltpu.roll(pf, 1, 0))
        mixed.append(pf + (shifted - pf) * mu_ref[:, off:off + width])
        if not has_state:
            carry_ref[0:1, off:off + width] = pf[RB - 1:RB, :]
    r, k, v, lo = mixed

    lo_wa = lo[:, 0:128]
    w_log = -_softplus(-(w0_ref[...] + _mm(jnp.tanh(lo_wa), w2_ref[...]))) - 0.5
    lw = -jnp.exp(w_log)
    a = _sigmoid(a0_ref[...] + _mm(lo_wa, a2_ref[...]))
    gate = _mm(_sigmoid(lo[:, 128:256]), g2_ref[...])
    kk = k * kk_ref[...]
    kmod = k * (1.0 + (a - 1.0) * ka_ref[...])
    kk = kk * lax.rsqrt(jnp.maximum(_seg_sum(kk * kk), 1e-24))
    kb = kk * a

    NB = 4 * R
    lanehead = _iota((1, 256), 1) >> 6
    bi, bj = _iota((NB, NB), 0), _iota((NB, NB), 1)
    same = _ind((bi >> lc) == (bj >> lc))
    strict = same * _ind(bj < bi)
    incl = same * _ind(bj <= bi)
    eye_nb = _ind(bi == bj)
    si, sj = _iota((R, R), 0), _iota((R, R), 1)
    seg = _ind((si >> lc) == (sj >> lc))
    seg_ones = seg.astype(BF16)
    seg_tri = (seg * _ind(sj <= si)).astype(BF16)

    def bd(x):
        parts = []
        for s in range(G):
            xs = x[s * C:(s + 1) * C]
            for h in range(4):
                parts.append(jnp.where(lanehead == h, xs, 0.0))
        return jnp.concatenate(parts, axis=0)

    chains = [(sub, hh) for sub in range(NSUB) for hh in range(2)]
    ops = []
    for sub, hh in chains:
        sr, sl = slice(sub * R, (sub + 1) * R), slice(hh * 256, (hh + 1) * 256)
        lw4 = lw[sr, sl]
        clw = _mm_exact_lhs(seg_tri, lw4)
        clt = _mm_exact_lhs(seg_ones, lw4)
        e_out = jnp.exp(-clw)
        e_end = jnp.exp(clt - clw)
        b_rq = bd(r[sr, sl] * jnp.exp(clw))
        ops.append(dict(
            kq=bd(kk[sr, sl] * jnp.exp(clw - lw4)).astype(BF16), rq=b_rq, rq16=b_rq.astype(BF16),
            kk=bd(kmod[sr, sl] * e_out).astype(BF16), bk=bd(kb[sr, sl] * e_out).astype(BF16),
            kk2=bd(kmod[sr, sl] * e_end).astype(BF16), bk2=bd(kb[sr, sl] * e_end).astype(BF16),
            v=bd(v[sr, sl]).astype(BF16), p_end=jnp.exp(clt)))

    n_mat = [strict * _mm_nt(o['kq'], o['bk']) for o in ops]
    a_kk = [strict * _mm_nt(o['kq'], o['kk']) for o in ops]
    a_rb = [incl * _mm_nt(o['rq16'], o['bk']) for o in ops]
    a_rk = [incl * _mm_nt(o['rq16'], o['kk']) for o in ops]

    t_inv = [eye_nb - n for n in n_mat]
    pw = n_mat
    for _ in range(lc - 1):
        pw = [_mm(x, x) for x in pw]
        t_inv = [t + _mm(t, x) for t, x in zip(t_inv, pw)]

    av = [_mm(a, o['v']) for a, o in zip(a_kk, ops)]
    wu = [_mm(t, jnp.concatenate([o['kq'], x.astype(BF16)], axis=1)) for t, o, x in zip(t_inv, ops, av)]
    arb_wu = [_mm(a, x) for a, x in zip(a_rb, wu)]
    rq2 = [o['rq'] - x[:, 0:256] for o, x in zip(ops, arb_wu)]
    y_loc = [_mm(a, o['v']) - x[:, 256:512] for a, o, x in zip(a_rk, ops, arb_wu)]
    wu16 = [x.astype(BF16) for x in wu]

    upd = []
    for i, o in enumerate(ops):
        per_seg = []
        for s in range(G):
            rs = slice(s * 4 * C, (s + 1) * 4 * C)
            bwq = _mm_tn(o['bk2'][rs], wu16[i][rs, 0:256])
            add = _mm_tn(o['v'][rs], o['kk2'][rs]) - _mm_tn(wu16[i][rs, 256:512], o['bk2'][rs])
            per_seg.append((bwq.astype(BF16), add))
        upd.append(per_seg)

    y_parts = {}
    for i, (sub, hh) in enumerate(chains):
        ys = []
        for s in range(G):
            rs = slice(s * 4 * C, (s + 1) * 4 * C)
            slot = sub * G + s if has_state else 0
            g0 = hs_ref[slot, hh]
            g0b = g0.astype(BF16)
            yb = _mm_nt(rq2[i][rs], g0b) + y_loc[i][rs]
            y4 = yb[0:C]
            for h in range(1, 4):
                y4 = y4 + yb[h * C:(h + 1) * C]
            ys.append(y4)
            bwq, add = upd[i][s]
            hs_ref[slot, hh] = g0 * ops[i]['p_end'][s * C:s * C + 1, :] - _mm_nt(g0b, bwq) + add
        y_parts[(sub, hh)] = ys[0] if G == 1 else jnp.concatenate(ys, axis=0)
    y_subs = [jnp.concatenate([y_parts[(sub, 0)], y_parts[(sub, 1)]], axis=1) for sub in range(NSUB)]
    y = y_subs[0] if NSUB == 1 else jnp.concatenate(y_subs, axis=0)

    mean = _seg_sum(y) * (1.0 / 64.0)
    yc = y - mean
    var = _seg_sum(yc * yc) * (1.0 / 64.0)
    yn = yc * lax.rsqrt(var + RWKV_GN_EPS) * lnw_ref[...] + lnb_ref[...]
    bonus = _seg_sum(r * kmod * rk_ref[...]) * v
    y_ref[...] = ((yn + bonus) * gate).astype(y_ref.dtype)

    @pl.when(c == pl.num_programs(1) - 1)
    def _():
        fold = _ind((_iota((256, 64), 0) & 63) == _iota((256, 64), 1)).astype(BF16)
        for slot in range(n_slots):
            for hh in range(2):
                hout_ref[slot, 4 * hh:4 * hh + 4] = _mm_exact_rhs(hs_ref[slot, hh], fold).reshape(4, 64, 64)


def _row(n):
    return pl.BlockSpec((1, n), lambda *_: (0, 0))


def _full(shape):
    nd = len(shape)
    return pl.BlockSpec(shape, lambda *_: (0,) * nd)


def _any():
    return pl.BlockSpec(memory_space=pl.ANY)


def _rmsnorm(x, g):
    t, tm = x.shape[0], TILES['norm_tm']
    return pl.pallas_call(
        _rmsnorm_kernel,
        grid=(t // tm,),
        in_specs=[pl.BlockSpec((tm, D_MODEL), lambda i: (i, 0)), _row(D_MODEL)],
        out_specs=pl.BlockSpec((tm, D_MODEL), lambda i: (i, 0)),
        out_shape=jax.ShapeDtypeStruct((t, D_MODEL), BF16),
        compiler_params=_cparams(("parallel",)),
        name="rmsnorm",
    )(x, g)


def _inproj(h, w_all, l):
    t, tm, tn = h.shape[0], TILES['in_tm'], TILES['in_tn']
    return pl.pallas_call(
        _inproj_kernel,
        grid=(t // tm, PW // tn),
        in_specs=[pl.BlockSpec((tm, D_MODEL), lambda i, j: (i, 0)),
                  pl.BlockSpec((None, D_MODEL, tn), lambda i, j: (l, 0, j))],
        out_specs=pl.BlockSpec((tm, tn), lambda i, j: (i, j)),
        out_shape=jax.ShapeDtypeStruct((t, PW), F32),
        compiler_params=_cparams(("parallel", "parallel")),
        name="inproj",
    )(h, w_all)


def _outproj(ys, w_all, l, x, g):
    t, tm = x.shape[0], TILES['out_tm']
    yspec = pl.BlockSpec((tm, GW), lambda i: (i, 0))
    xspec = pl.BlockSpec((tm, D_MODEL), lambda i: (i, 0))
    return pl.pallas_call(
        _outproj_kernel,
        grid=(t // tm,),
        in_specs=[yspec, yspec, yspec, yspec,
                  pl.BlockSpec((None, D_MODEL, D_MODEL), lambda i: (l, 0, 0)), xspec, _row(D_MODEL)],
        out_specs=[xspec, xspec],
        out_shape=[jax.ShapeDtypeStruct((t, D_MODEL), F32), jax.ShapeDtypeStruct((t, D_MODEL), BF16)],
        compiler_params=_cparams(("parallel",)),
        name="outproj",
    )(*ys, w_all, x, g)


def _mlp(h, w1_all, w2_all, l, x, g, final):
    t, tm, tf = x.shape[0], TILES['mlp_tm'], TILES['mlp_tf']
    xspec = pl.BlockSpec((tm, D_MODEL), lambda i, k: (i, 0))
    out_specs = [xspec] if final else [xspec, xspec]
    out_shape = [jax.ShapeDtypeStruct((t, D_MODEL), F32)]
    if not final:
        out_shape.append(jax.ShapeDtypeStruct((t, D_MODEL), BF16))
    return pl.pallas_call(
        functools.partial(_mlp_kernel, final=final),
        grid=(t // tm, D_FF // tf),
        in_specs=[xspec,
                  pl.BlockSpec((None, D_MODEL, tf), lambda i, k: (l, 0, k)),
                  pl.BlockSpec((None, tf, D_MODEL), lambda i, k: (l, k, 0)),
                  xspec, _row(D_MODEL)],
        out_specs=out_specs,
        out_shape=out_shape,
        compiler_params=_cparams(("parallel", "arbitrary")),
        name="mlp",
    )(h, w1_all, w2_all, x, g)


def _mixer_call(kernel_fn, name, p, t_rows, n_seq, n_blk, rows, row0, col_specs, state_in, params, y_prev,
                state_out, scratch, tables=()):
    def rowmap(cb):
        return lambda s, c: (row0 + s * n_blk + c, cb)

    in_specs = [pl.BlockSpec((rows, w), rowmap(cb)) for w, cb in col_specs]
    args = [p] * len(col_specs)
    for arr, blk in state_in:
        nd = len(blk)
        in_specs.append(pl.BlockSpec(blk, lambda s, c, nd=nd: (s,) + (0,) * (nd - 1)))
        args.append(arr)
    for tab, blk in tables:
        in_specs.append(pl.BlockSpec(blk, lambda s, c: (c, 0)))
        args.append(tab)
    for arr in params:
        in_specs.append(_full(arr.shape))
        args.append(arr)
    aliases = {}
    if y_prev is not None:
        aliases = {len(args): 0}
        in_specs.append(_any())
        args.append(y_prev)
    out_specs = [pl.BlockSpec((rows, GW), lambda s, c: (row0 + s * n_blk + c, 0))]
    out_shape = [jax.ShapeDtypeStruct((t_rows, GW), BF16)]
    for shape, blk in state_out:
        nd = len(blk)
        out_specs.append(pl.BlockSpec(blk, lambda s, c, nd=nd: (s,) + (0,) * (nd - 1)))
        out_shape.append(jax.ShapeDtypeStruct(shape, F32))
    return pl.pallas_call(
        kernel_fn,
        grid=(n_seq, n_blk),
        in_specs=in_specs,
        out_specs=out_specs,
        out_shape=out_shape,
        scratch_shapes=scratch,
        input_output_aliases=aliases,
        compiler_params=_cparams(("arbitrary", "arbitrary")),
        name=name,
    )(*args)


def _run_ssm(p, dims, params, buf_s, h0_s):
    t, bp, lp, bs, ls = dims
    tp = bp * lp
    q = TILES['ssm_q']
    cols = [(SSM_XBC, CB_XBC), (GW, CB_Z), (128, CB_DT)]
    hshape = lambda n: [((n, SSM_HEADS, 64, SSM_STATE), (1, SSM_HEADS, 64, SSM_STATE))]
    scratch = lambda rows: [pltpu.VMEM((HIST + rows, SSM_XBC), F32), pltpu.VMEM((2, SSM_STATE, 256), F32)]
    y, h_p = _mixer_call(functools.partial(_ssm_kernel, Q=q, has_state=False), "ssm_prompt", p, t, bp, lp // q, q,
                         0, cols, [], params, None, hshape(bp), scratch(q))
    y, h_s = _mixer_call(functools.partial(_ssm_kernel, Q=ls, has_state=True), "ssm_sample", p, t, bs, 1, ls,
                         tp // ls, cols, [(buf_s, (1, HIST, SSM_XBC)), (h0_s, (1, SSM_HEADS, 64, SSM_STATE))],
                         params, y, hshape(bs), scratch(ls))
    return y, h_p, h_s


def _run_lru(p, dims, params, buf_s, h0_s):
    t, bp, lp, bs, ls = dims
    tp = bp * lp
    r = TILES['lru_r']
    cols = [(2 * GW, CB_LRU)]
    hshape = lambda n: [((n, 1, GW), (1, 1, GW))]
    scratch = lambda rows: [pltpu.VMEM((HIST + rows, GW), F32), pltpu.VMEM((1, GW), F32)]
    y, h_p = _mixer_call(functools.partial(_lru_kernel, R=r, has_state=False), "lru_prompt", p, t, bp, lp // r, r, 0,
                         cols, [], params, None, hshape(bp), scratch(r))
    y, h_s = _mixer_call(functools.partial(_lru_kernel, R=ls, has_state=True), "lru_sample", p, t, bs, 1, ls,
                         tp // ls, cols, [(buf_s, (1, HIST, GW)), (h0_s, (1, 1, GW))], params, y, hshape(bs),
                         scratch(ls))
    return y, h_p[:, 0], h_s[:, 0]


def _run_swa(p, dims, rope_p, rope_s, sinks, k0_s, v0_s):
    t, bp, lp, bs, ls = dims
    tp = bp * lp
    q, ns = TILES['swa_q'], TILES['swa_ns']
    cols = [(GW, CB_Q), (256, CB_KV)]
    cshape = lambda n, g: [((n, WINDOW, 128), (g, WINDOW, 128))] * 2
    scratch = lambda g: [pltpu.VMEM((g, WINDOW, 128), F32), pltpu.VMEM((g, WINDOW, 128), F32)]
    y, k_p, v_p = _mixer_call(functools.partial(_swa_kernel, QB=q, NS=1, has_state=False), "swa_prompt", p, t, bp,
                              lp // q, q, 0, cols, [], [sinks], None, cshape(bp, 1), scratch(1),
                              tables=[(tab, (q, GW)) for tab in rope_p])
    y, k_s, v_s = _mixer_call(functools.partial(_swa_kernel, QB=ls, NS=ns, has_state=True), "swa_sample", p, t,
                              bs // ns, 1, ns * ls, tp // (ns * ls), cols,
                              [(k0_s, (ns, WINDOW, 128)), (v0_s, (ns, WINDOW, 128))], [sinks], y, cshape(bs, ns),
                              scratch(ns), tables=[(tab, (ls, GW)) for tab in rope_s])
    return y, k_p, v_p, k_s, v_s


def _run_rwkv(p, dims, params, sh_rows_s, s0_s):
    t, bp, lp, bs, ls = dims
    tp = bp * lp
    c, nsub = TILES['rwkv_c'], TILES['rwkv_sub']
    g, nsub_s = TILES['rwkv_g'], TILES['rwkv_sub_s']
    cols = [(GW, CB_R), (GW, CB_K), (GW, CB_V), (256, CB_LORA)]
    sshape = lambda n, blk: [((n, 8, 64, 64), (blk, 8, 64, 64))]
    scratch = lambda slots: [pltpu.VMEM((slots, 2, 256, 256), F32), pltpu.VMEM((HIST, RWKV_COLS), F32)]
    rows_p = nsub * c
    y, s_p = _mixer_call(functools.partial(_rwkv_kernel, NSUB=nsub, G=1, C=c, has_state=False), "rwkv_prompt", p, t,
                         bp, lp // rows_p, rows_p, 0, cols, [], params, None, sshape(bp, 1), scratch(1))
    nseq = nsub_s * g
    rows_s = nseq * ls
    y, s_s = _mixer_call(functools.partial(_rwkv_kernel, NSUB=nsub_s, G=g, C=ls, has_state=True), "rwkv_sample", p, t,
                         bs // nseq, 1, rows_s, tp // rows_s, cols,
                         [(sh_rows_s, (rows_s, RWKV_COLS)), (s0_s, (nseq, 8, 64, 64))], params, y, sshape(bs, nseq),
                         scratch(nseq))
    return y, s_p, s_s


def _rope_tables(pos):
    half = ROT_DIM // 2
    inv = ROPE_THETA ** (-(jnp.arange(half, dtype=F32) * 2.0) / ROT_DIM)
    ang = pos.astype(F32)[:, None] * inv
    cos, sin = jnp.cos(ang), jnp.sin(ang)
    n = pos.shape[0]
    zeros = jnp.zeros((n, half), F32)
    rest = SWA_HEADDIM - ROT_DIM
    cos64 = jnp.concatenate([cos, cos, jnp.ones((n, rest), F32)], axis=1)
    sa64 = jnp.concatenate([-sin, zeros, jnp.zeros((n, rest), F32)], axis=1)
    sb64 = jnp.concatenate([zeros, sin, jnp.zeros((n, rest), F32)], axis=1)
    return tuple(jnp.tile(t, (1, GW // SWA_HEADDIM)) for t in (cos64, sa64, sb64))


def _pad_rows_front(x, axis, total):
    pad = [(0, 0)] * x.ndim
    pad[axis] = (total - x.shape[axis], 0)
    return jnp.pad(x, pad)


def kernel(x_prompt, x_sample, state_ssm, state_ssm_conv, state_rwkv, state_rwkv_shift, cache_swa_k,
           cache_swa_v, state_lru, state_lru_conv, norm_mix, w_in, ssm_conv_w, ssm_conv_b, ssm_dt_bias,
           ssm_a_log, ssm_d, ssm_norm, rwkv_mu, rwkv_w0, rwkv_w2, rwkv_a0, rwkv_a2, rwkv_g2, rwkv_k_k,
           rwkv_k_a, rwkv_r_k, rwkv_ln_w, rwkv_ln_b, swa_sinks, lru_conv_w, lru_conv_b, lru_wa, lru_ba,
           lru_wi, lru_bi, lru_lambda, w_out, norm_mlp, mlp_w1, mlp_w2, norm_final):
    bp, lp, _ = x_prompt.shape
    bs, ls, _ = x_sample.shape
    tp, ts = bp * lp, bs * ls
    t = tp + ts
    dims = (t, bp, lp, bs, ls)
    depth = w_in.shape[0]

    o_rwkv = GW + SSM_XBC + SSM_HEADS
    o_swa = o_rwkv + RWKV_COLS
    o_lru = o_swa + GW + 256
    w_in_r = jnp.concatenate([
        w_in[..., GW:GW + SSM_XBC],
        w_in[..., o_lru:o_lru + 2 * GW],
        w_in[..., 0:GW],
        w_in[..., o_rwkv:o_rwkv + 3 * GW],
        w_in[..., o_swa:o_swa + GW],
        w_in[..., o_swa + GW:o_swa + GW + 256],
        w_in[..., o_rwkv + 3 * GW:o_rwkv + RWKV_COLS],
        w_in[..., GW + SSM_XBC:GW + SSM_XBC + SSM_HEADS],
        jnp.zeros(w_in.shape[:2] + (PW - (o_lru + 2 * GW),), w_in.dtype),
    ], axis=-1).astype(BF16)
    w_out_b = w_out.astype(BF16)
    w1_b = mlp_w1.astype(BF16)
    w2_b = mlp_w2.astype(BF16)
    eye8 = jnp.eye(8, dtype=F32)
    lru_wa_d = jnp.einsum('lncd,nm->lncmd', lru_wa, eye8).reshape(depth, GW, GW).astype(BF16)
    lru_wi_d = jnp.einsum('lncd,nm->lncmd', lru_wi, eye8).reshape(depth, GW, GW).astype(BF16)
    z64 = jnp.zeros((depth, 64, GW), F32)
    rwkv_w2_p = jnp.concatenate([rwkv_w2, z64], axis=1).astype(BF16)
    rwkv_a2_p = jnp.concatenate([z64, rwkv_a2], axis=1).astype(BF16)
    rwkv_g2_b = rwkv_g2.astype(BF16)
    dtb_p = jnp.pad(ssm_dt_bias, ((0, 0), (0, 128 - SSM_HEADS)))
    alog_p = jnp.pad(ssm_a_log, ((0, 0), (0, 128 - SSM_HEADS)))
    d_e = jnp.repeat(ssm_d, 64, axis=-1)
    sinks_b = jnp.broadcast_to(swa_sinks[:, :, None], (depth, 8, 128))
    rope_p = _rope_tables(jnp.arange(lp))
    rope_s = _rope_tables(PAST_LEN + jnp.arange(ls))

    def r2(a, l):
        return a[l][None, :]

    ssm_buf = _pad_rows_front(state_ssm_conv, 2, HIST)
    lru_buf = _pad_rows_front(state_lru_conv, 2, HIST)
    lru_h0 = state_lru[:, :, None, :]
    sh_rows = jnp.pad(state_rwkv_shift[:, :, None, :], ((0, 0), (0, 0), (0, ls - 1), (0, 0))).reshape(depth, ts, RWKV_COLS)
    swa_k0 = cache_swa_k.reshape(depth, bs, WINDOW, 128)
    swa_v0 = cache_swa_v.reshape(depth, bs, WINDOW, 128)

    x = jnp.concatenate([x_prompt.reshape(tp, D_MODEL), x_sample.reshape(ts, D_MODEL)], axis=0)
    h = _rmsnorm(x, r2(norm_mix, 0))

    names = ('ssm_p', 'ssm_s', 'conv_p', 'conv_s', 'rwkv_p', 'rwkv_s', 'shift_p', 'shift_s',
             'k_p', 'k_s', 'v_p', 'v_s', 'lru_p', 'lru_s', 'lconv_p', 'lconv_s')
    outs = {n: [] for n in names}
    y_final = None
    for l in range(depth):
        p = _inproj(h, w_in_r, l)

        ssm_params = [ssm_conv_w[l], r2(ssm_conv_b, l), r2(dtb_p, l), r2(alog_p, l), r2(d_e, l), r2(ssm_norm, l)]
        y_ssm, hs_p, hs_s = _run_ssm(p, dims, ssm_params, ssm_buf[l], state_ssm[l])
        rwkv_params = [r2(rwkv_mu, l), r2(rwkv_w0, l), rwkv_w2_p[l], r2(rwkv_a0, l), rwkv_a2_p[l], rwkv_g2_b[l],
                       r2(rwkv_k_k, l), r2(rwkv_k_a, l), rwkv_r_k[l].reshape(1, GW), r2(rwkv_ln_w, l),
                       r2(rwkv_ln_b, l)]
        y_rwkv, s_p, s_s = _run_rwkv(p, dims, rwkv_params, sh_rows[l], state_rwkv[l])
        y_swa, k_p, v_p, k_s, v_s = _run_swa(p, dims, rope_p, rope_s, sinks_b[l], swa_k0[l], swa_v0[l])
        lru_params = [lru_conv_w[l], r2(lru_conv_b, l), lru_wa_d[l], r2(lru_ba, l), lru_wi_d[l], r2(lru_bi, l),
                      r2(lru_lambda, l)]
        y_lru, lh_p, lh_s = _run_lru(p, dims, lru_params, lru_buf[l], lru_h0[l])

        x1, h2 = _outproj([y_ssm, y_rwkv, y_swa, y_lru], w_out_b, l, x, r2(norm_mlp, l))
        if l + 1 < depth:
            x, h = _mlp(h2, w1_b, w2_b, l, x1, r2(norm_mix, l + 1), False)
        else:
            (y_final,) = _mlp(h2, w1_b, w2_b, l, x1, norm_final[None, :], True)

        def tail_p(n, c0, c1):
            return jnp.stack([p[b * lp + lp - n:b * lp + lp, c0:c1] for b in range(bp)], axis=0)

        ps = p[tp:].reshape(bs, ls, PW)
        lora0 = CB_LORA * 256
        outs['ssm_p'].append(hs_p)
        outs['ssm_s'].append(hs_s)
        outs['conv_p'].append(tail_p(3, 0, SSM_XBC))
        outs['conv_s'].append(ps[:, ls - 3:, 0:SSM_XBC])
        outs['rwkv_p'].append(s_p)
        outs['rwkv_s'].append(s_s)
        outs['shift_p'].append(jnp.concatenate([tail_p(1, 5 * GW, 8 * GW), tail_p(1, lora0, lora0 + 256)], axis=-1)[:, 0])
        outs['shift_s'].append(jnp.concatenate([ps[:, ls - 1, 5 * GW:8 * GW], ps[:, ls - 1, lora0:lora0 + 256]], axis=-1))
        outs['k_p'].append(k_p.reshape(bp, WINDOW, 2, SWA_HEADDIM))
        outs['k_s'].append(k_s.reshape(bs, WINDOW, 2, SWA_HEADDIM))
        outs['v_p'].append(v_p.reshape(bp, WINDOW, 2, SWA_HEADDIM))
        outs['v_s'].append(v_s.reshape(bs, WINDOW, 2, SWA_HEADDIM))
        outs['lru_p'].append(lh_p)
        outs['lru_s'].append(lh_s)
        outs['lconv_p'].append(tail_p(3, SSM_XBC, SSM_XBC + GW))
        outs['lconv_s'].append(ps[:, ls - 3:, SSM_XBC:SSM_XBC + GW])

    st = {n: jnp.stack(v, axis=0) for n, v in outs.items()}
    return (y_final[:tp].reshape(bp, lp, D_MODEL), y_final[tp:].reshape(bs, ls, D_MODEL),
            st['ssm_p'], st['ssm_s'], st['conv_p'], st['conv_s'], st['rwkv_p'], st['rwkv_s'],
            st['shift_p'], st['shift_s'], st['k_p'], st['k_s'], st['v_p'], st['v_s'],
            st['lru_p'], st['lru_s'], st['lconv_p'], st['lconv_s'])
```

```python
import functools
import math

import jax
import jax.numpy as jnp
from jax import lax
from jax.experimental import pallas as pl
from jax.experimental.pallas import tpu as pltpu

F32 = jnp.float32
BF16 = jnp.bfloat16

D_MODEL = 2048
GW = 512
RMS_EPS = 1e-6
CONV_W = 4
SSM_HEADS = 8
SSM_STATE = 128
SSM_XBC = 1024
RWKV_COLS = 1792
RWKV_GN_EPS = 64e-5
SWA_HEADDIM = 64
WINDOW = 128
ROT_DIM = 16
ROPE_THETA = 500000.0
LRU_C = 8.0
D_FF = 4 * D_MODEL
PAST_LEN = 16384

PW = 5376
CB_XBC, CB_LRU = 0, 1
CB_Z, CB_R, CB_K, CB_V, CB_Q = 4, 5, 6, 7, 8
CB_KV, CB_LORA = 18, 19
CB_DT = 40

HIST = 8
VMEM_LIMIT = 56 * 1024 * 1024

TILES = dict(
    prep_tk=256, in_tm=1408, in_tn=768, out_tm=704, mlp_tm=704, mlp_tf=1024,
    ssm_q=128, swa_q=128, lru_r=256, rwkv_c=64, rwkv_sub=4,
    swa_ns=4, rwkv_g=4, rwkv_sub_s=2,
)


def _cparams(sem):
    return pltpu.CompilerParams(dimension_semantics=sem, vmem_limit_bytes=VMEM_LIMIT)


def _mm(a, b):
    return jnp.dot(a.astype(BF16), b.astype(BF16), preferred_element_type=F32)


def _mm_nt(a, b):
    return lax.dot_general(a.astype(BF16), b.astype(BF16), (((1,), (1,)), ((), ())),
                           preferred_element_type=F32)


def _mm_tn(a, b):
    return lax.dot_general(a.astype(BF16), b.astype(BF16), (((0,), (0,)), ((), ())),
                           preferred_element_type=F32)


def _split3(x):
    x1 = x.astype(BF16)
    r = x - x1.astype(F32)
    x2 = r.astype(BF16)
    r = r - x2.astype(F32)
    return x1, x2, r.astype(BF16)


def _mm_exact_rhs(a, sel):
    p1, p2, p3 = _split3(a)
    return _mm(p1, sel) + _mm(p2, sel) + _mm(p3, sel)


def _mm_split2_rhs(a, sel):
    p1 = a.astype(BF16)
    p2 = (a - p1.astype(F32)).astype(BF16)
    return _mm(p1, sel) + _mm(p2, sel)


def _mm_exact_lhs(sel, b):
    p1, p2, p3 = _split3(b)
    return _mm(sel, p1) + _mm(sel, p2) + _mm(sel, p3)


def _mm_nt_exact_lhs(sel, b):
    p1, p2, p3 = _split3(b)
    return _mm_nt(sel, p1) + _mm_nt(sel, p2) + _mm_nt(sel, p3)


def _softplus(x):
    return jnp.maximum(x, 0.0) + jnp.log1p(jnp.exp(-jnp.abs(x)))


def _sigmoid(x):
    return 1.0 / (1.0 + jnp.exp(-x))


def _iota(shape, dim):
    return lax.broadcasted_iota(jnp.int32, shape, dim)


def _ind(mask):
    return jnp.where(mask, 1.0, 0.0)


def _rms(x, g):
    ms = jnp.mean(x * x, axis=-1, keepdims=True)
    return x * lax.rsqrt(ms + RMS_EPS) * g


def _conv_from_hist(ext_ref, rows, cw_ref, cb_ref):
    acc = cb_ref[...] + cw_ref[0:1, :] * ext_ref[HIST - 3:HIST - 3 + rows, :]
    for j in range(1, CONV_W):
        acc = acc + cw_ref[j:j + 1, :] * ext_ref[HIST - 3 + j:HIST - 3 + j + rows, :]
    return acc


def _winprep_kernel(w_ref, o_ref):
    o_rwkv = GW + SSM_XBC + SSM_HEADS
    o_swa = o_rwkv + RWKV_COLS
    o_lru = o_swa + GW + 256
    sections = ((GW, SSM_XBC), (o_lru, 2 * GW), (0, GW), (o_rwkv, 3 * GW), (o_swa, GW), (o_swa + GW, 256),
                (o_rwkv + 3 * GW, 256), (GW + SSM_XBC, SSM_HEADS))
    dst = 0
    for src, width in sections:
        o_ref[:, dst:dst + width] = w_ref[:, src:src + width].astype(BF16)
        dst += width
    o_ref[:, dst:PW] = jnp.zeros((o_ref.shape[0], PW - dst), BF16)


def _inproj_kernel(x_ref, g_ref, w_ref, o_ref, h_ref):
    @pl.when(pl.program_id(1) == 0)
    def _():
        h_ref[...] = _rms(x_ref[...], g_ref[...]).astype(BF16)

    o_ref[...] = jnp.dot(h_ref[...], w_ref[...], preferred_element_type=F32)


def _outproj_kernel(y0_ref, y1_ref, y2_ref, y3_ref, w_ref, x_ref, x1_ref):
    acc = x_ref[...]
    for i, y_ref in enumerate((y0_ref, y1_ref, y2_ref, y3_ref)):
        acc = acc + jnp.dot(y_ref[...], w_ref[i * GW:(i + 1) * GW, :], preferred_element_type=F32)
    x1_ref[...] = acc


def _mlp_kernel(x_ref, g_ref, w1_ref, w2_ref, gf_ref, xo_ref, h_ref, *, final):
    k = pl.program_id(1)

    @pl.when(k == 0)
    def _():
        x = x_ref[...]
        h_ref[...] = _rms(x, g_ref[...]).astype(BF16)
        xo_ref[...] = x

    a = jnp.dot(h_ref[...], w1_ref[...], preferred_element_type=F32)
    a = jnp.square(jnp.maximum(a, 0.0)).astype(BF16)
    xo_ref[...] += jnp.dot(a, w2_ref[...], preferred_element_type=F32)

    if final:
        @pl.when(k == pl.num_programs(1) - 1)
        def _():
            xo_ref[...] = _rms(xo_ref[...], gf_ref[...])


def _ssm_kernel(*refs, Q, has_state):
    if has_state:
        (xbc_ref, z_ref, dt_ref, buf_ref, h0_ref, cw_ref, cb_ref, dtb_ref, alog_ref, de_ref, nw_ref,
         _, y_ref, hout_ref, ext_ref, h_ref) = refs
    else:
        (xbc_ref, z_ref, dt_ref, cw_ref, cb_ref, dtb_ref, alog_ref, de_ref, nw_ref,
         y_ref, hout_ref, ext_ref, h_ref) = refs
    c = pl.program_id(1)

    @pl.when(c == 0)
    def _():
        if has_state:
            ext_ref[0:HIST, :] = buf_ref[0]
            for g in range(2):
                h_ref[g] = h0_ref[0, 4 * g:4 * g + 4].reshape(256, SSM_STATE).T
        else:
            ext_ref[0:HIST, :] = jnp.zeros((HIST, SSM_XBC), F32)
            h_ref[...] = jnp.zeros(h_ref.shape, F32)

    ext_ref[HIST:HIST + Q, :] = xbc_ref[...]
    pre = _conv_from_hist(ext_ref, Q, cw_ref, cb_ref)
    ext_ref[0:HIST, :] = ext_ref[Q:Q + HIST, :]
    xbc = pre * _sigmoid(pre)
    x = xbc[:, 0:GW]
    bm = xbc[:, GW:GW + 256]
    cm = xbc[:, GW + 256:GW + 512]

    dt = _softplus(dt_ref[...] + dtb_ref[...])
    a_neg = jnp.where(_iota((1, 128), 1) < SSM_HEADS, -jnp.exp(alog_ref[...]), 0.0)
    tri = (_iota((Q, Q), 0) >= _iota((Q, Q), 1))
    cum = _mm_exact_lhs(_ind(tri).astype(BF16), dt * a_neg)
    expand = _ind(_iota((128, GW), 0) == (_iota((128, GW), 1) >> 6)).astype(BF16)
    dt_e = _mm_exact_rhs(dt, expand)
    cum_e = _mm_exact_rhs(cum, expand)
    sel = _ind(_iota((SSM_HEADS, 128), 0) == _iota((SSM_HEADS, 128), 1)).astype(BF16)
    cum_t = _mm_nt_exact_lhs(sel, cum)
    last = cum_e[Q - 1:Q, :]
    expc_e = jnp.exp(cum_e)
    tail_e = jnp.exp(last - cum_e)
    dtot_e = jnp.exp(last)
    dx = dt_e * x
    lanehead = _iota((1, 256), 1) >> 6

    ys = []
    for g in range(2):
        bg = bm[:, g * 128:(g + 1) * 128]
        cg = cm[:, g * 128:(g + 1) * 128]
        cb = _mm_nt(cg, bg)
        h_t = h_ref[g]
        yg = _mm(cg, h_t) * expc_e[:, g * 256:(g + 1) * 256]
        dxg = dx[:, g * 256:(g + 1) * 256]
        for hh in range(4):
            h = 4 * g + hh
            decay = jnp.exp(jnp.minimum(cum[:, h:h + 1] - cum_t[h:h + 1, :], 0.0))
            m = jnp.where(tri, cb * decay, 0.0)
            yg = yg + _mm(m, jnp.where(lanehead == hh, dxg, 0.0))
        ys.append(yg)
        h_ref[g] = (h_t * dtot_e[:, g * 256:(g + 1) * 256]
                    + _mm_tn(bg, dxg * tail_e[:, g * 256:(g + 1) * 256]))

    y = jnp.concatenate(ys, axis=1) + de_ref[...] * x
    z = z_ref[...]
    y = y * (z * _sigmoid(z))
    outs = []
    for g in range(2):
        yg = y[:, g * 256:(g + 1) * 256]
        outs.append(yg * lax.rsqrt(jnp.mean(yg * yg, axis=-1, keepdims=True) + RMS_EPS))
    y_ref[...] = (jnp.concatenate(outs, axis=1) * nw_ref[...]).astype(y_ref.dtype)

    @pl.when(c == pl.num_programs(1) - 1)
    def _():
        for g in range(2):
            hout_ref[0, 4 * g:4 * g + 4] = h_ref[g].T.reshape(4, 64, SSM_STATE)


def _lru_kernel(*refs, R, has_state):
    if has_state:
        (u_ref, buf_ref, h0_ref, cw_ref, cb_ref, wa_ref, ba_ref, wi_ref, bi_ref, lam_ref,
         _, y_ref, hout_ref, ext_ref, h_ref) = refs
    else:
        (u_ref, cw_ref, cb_ref, wa_ref, ba_ref, wi_ref, bi_ref, lam_ref,
         y_ref, hout_ref, ext_ref, h_ref) = refs
    c = pl.program_id(1)

    @pl.when(c == 0)
    def _():
        if has_state:
            ext_ref[0:HIST, :] = buf_ref[0]
            h_ref[...] = h0_ref[0]
        else:
            ext_ref[0:HIST, :] = jnp.zeros((HIST, GW), F32)
            h_ref[...] = jnp.zeros(h_ref.shape, F32)

    u = u_ref[...]
    gate = u[:, GW:]
    ext_ref[HIST:HIST + R, :] = u[:, :GW]
    xc = _conv_from_hist(ext_ref, R, cw_ref, cb_ref)
    ext_ref[0:HIST, :] = ext_ref[R:R + HIST, :]

    rg = _sigmoid(_mm(xc, wa_ref[...]) + ba_ref[...])
    ig = _sigmoid(_mm(xc, wi_ref[...]) + bi_ref[...])
    log_a = -LRU_C * rg * _softplus(-lam_ref[...])
    a = jnp.exp(log_a)
    th = jnp.tanh(log_a)
    b = jnp.sqrt(-2.0 * th / (1.0 - th)) * (ig * xc)

    row = _iota((R, 1), 0)
    s = 1
    while s < R:
        keep = row >= s
        a_prev = jnp.where(keep, pltpu.roll(a, s, 0), 1.0)
        b_prev = jnp.where(keep, pltpu.roll(b, s, 0), 0.0)
        b = b + a * b_prev
        a = a * a_prev
        s *= 2
    h = b + a * h_ref[...]
    h_ref[...] = h[R - 1:R, :]
    hout_ref[0] = h[R - 1:R, :]

    gelu = 0.5 * gate * (1.0 + jnp.tanh(math.sqrt(2.0 / math.pi) * (gate + 0.044715 * (gate * gate * gate))))
    y_ref[...] = (h * gelu).astype(y_ref.dtype)


def _swa_kernel(*refs, QB, NS, has_state):
    if has_state:
        (q_ref, kv_ref, kc_ref, vc_ref, cos_ref, sa_ref, sb_ref, sink_ref,
         _, y_ref, kout_ref, vout_ref, hk_ref, hv_ref) = refs
    else:
        (q_ref, kv_ref, cos_ref, sa_ref, sb_ref, sink_ref,
         y_ref, kout_ref, vout_ref, hk_ref, hv_ref) = refs
    c = pl.program_id(1)
    nk = WINDOW + QB
    lq = QB.bit_length() - 1

    @pl.when(c == 0)
    def _():
        if has_state:
            hk_ref[...] = kc_ref[...]
            hv_ref[...] = vc_ref[...]
        else:
            hk_ref[...] = jnp.zeros(hk_ref.shape, F32)
            hv_ref[...] = jnp.zeros(hv_ref.shape, F32)

    def rope(x, width):
        cos, sa, sb = cos_ref[:, 0:width], sa_ref[:, 0:width], sb_ref[:, 0:width]
        return x * cos + pltpu.roll(x, width - ROT_DIM // 2, 1) * sa + pltpu.roll(x, ROT_DIM // 2, 1) * sb

    kj = _iota((nk, 4 * QB), 0)
    qi = _iota((nk, 4 * QB), 1) & (QB - 1)
    lo = qi if has_state else jnp.where(c == 0, jnp.maximum(qi, WINDOW), qi)
    neg_mask = jnp.where(kj >= lo, jnp.where(kj <= qi + WINDOW, 0.0, -jnp.inf), -jnp.inf)
    upper = _iota((1, 128), 1) >= SWA_HEADDIM
    lower = _iota((1, 128), 1) < SWA_HEADDIM
    chead = _iota((1, 4 * QB), 1) >> lq
    sink_rows = []
    for g in range(2):
        sr = sink_ref[4 * g + 3:4 * g + 4, 0:1]
        for hh in (2, 1, 0):
            sr = jnp.where(chead == hh, sink_ref[4 * g + hh:4 * g + hh + 1, 0:1], sr)
        sink_rows.append(sr)

    for s in range(NS):
        rows = slice(s * QB, (s + 1) * QB)
        q = rope(q_ref[rows, :], GW) * (SWA_HEADDIM ** -0.5)
        kv = kv_ref[rows, :]
        k_new = rope(kv[:, 0:128], 128)
        v_new = kv[:, 128:256]
        keys = jnp.concatenate([hk_ref[s], k_new], axis=0)
        vals = jnp.concatenate([hv_ref[s], v_new], axis=0)

        tiles = [None] * 4
        for g in range(2):
            own = upper if g == 1 else lower
            kd = jnp.where(own, keys, pltpu.roll(keys, SWA_HEADDIM, 1))
            vd = jnp.where(own, vals, pltpu.roll(vals, SWA_HEADDIM, 1))
            parts = []
            for hh in range(4):
                h = 4 * g + hh
                mine = upper if h % 2 == 1 else lower
                parts.append(jnp.where(mine, q[:, 128 * (h // 2):128 * (h // 2 + 1)], 0.0))
            sc = _mm_nt(kd, jnp.concatenate(parts, axis=0)) + neg_mask
            m = jnp.maximum(jnp.max(sc, axis=0, keepdims=True), sink_rows[g])
            e = jnp.exp(sc - m)
            den = jnp.sum(e, axis=0, keepdims=True) + jnp.exp(sink_rows[g] - m)
            o = _mm_tn(e / den, vd)
            for hh in range(4):
                h = 4 * g + hh
                oh = o[hh * QB:(hh + 1) * QB]
                tiles[h // 2] = oh if h % 2 == 0 else jnp.where(upper, oh, tiles[h // 2])
        y_ref[rows, :] = jnp.concatenate(tiles, axis=1).astype(y_ref.dtype)

        if QB == WINDOW:
            hk_ref[s] = k_new
            hv_ref[s] = v_new
        else:
            hk_ref[s] = keys[QB:, :]
            hv_ref[s] = vals[QB:, :]
    kout_ref[...] = hk_ref[...]
    vout_ref[...] = hv_ref[...]


def _seg_sum(x):
    ones = _ind((_iota((256, 256), 0) >> 6) == (_iota((256, 256), 1) >> 6)).astype(BF16)
    return jnp.concatenate([_mm_split2_rhs(x[:, 0:256], ones), _mm_split2_rhs(x[:, 256:512], ones)], axis=1)


def _rwkv_kernel(*refs, NSUB, G, C, has_state):
    if has_state:
        (r_ref, k_ref, v_ref, lo_ref, sh_ref, h0_ref, mu_ref, w0_ref, w2_ref, a0_ref, a2_ref, g2_ref,
         kk_ref, ka_ref, rk_ref, lnw_ref, lnb_ref, _, y_ref, hout_ref, hs_ref, carry_ref) = refs
    else:
        (r_ref, k_ref, v_ref, lo_ref, mu_ref, w0_ref, w2_ref, a0_ref, a2_ref, g2_ref,
         kk_ref, ka_ref, rk_ref, lnw_ref, lnb_ref, y_ref, hout_ref, hs_ref, carry_ref) = refs
    c = pl.program_id(1)
    R = G * C
    RB = NSUB * R
    lc = C.bit_length() - 1
    n_slots = NSUB * G if has_state else 1
    blockmask = _ind((_iota((256, 256), 0) >> 6) == (_iota((256, 256), 1) >> 6))

    @pl.when(c == 0)
    def _():
        if has_state:
            unfold = _ind(_iota((64, 256), 0) == (_iota((64, 256), 1) & 63)).astype(BF16)
            for slot in range(n_slots):
                for hh in range(2):
                    s_flat = h0_ref[slot, 4 * hh:4 * hh + 4].reshape(256, 64)
                    hs_ref[slot, hh] = blockmask * _mm_exact_rhs(s_flat, unfold)
        else:
            hs_ref[...] = jnp.zeros(hs_ref.shape, F32)
            carry_ref[...] = jnp.zeros(carry_ref.shape, F32)

    row = _iota((RB, 1), 0)
    first = ((row & (C - 1)) == 0) if has_state else (row == 0)
    mixed = []
    for ref, off, width in ((r_ref, 0, GW), (k_ref, GW, GW), (v_ref, 2 * GW, GW), (lo_ref, 3 * GW, 256)):
        pf = ref[...]
        if has_state:
            fv = sh_ref[:, off:off + width]
        else:
            fv = carry_ref[0:1, off:off + width]
        shifted = jnp.where(first, fv, pltpu.roll(pf, 1, 0))
        mixed.append(pf + (shifted - pf) * mu_ref[:, off:off + width])
        if not has_state:
            carry_ref[0:1, off:off + width] = pf[RB - 1:RB, :]
    r, k, v, lo = mixed

    lo_wa = lo[:, 0:128]
    w_log = -_softplus(-(w0_ref[...] + _mm(jnp.tanh(lo_wa), w2_ref[...]))) - 0.5
    lw = -jnp.exp(w_log)
    a = _sigmoid(a0_ref[...] + _mm(lo_wa, a2_ref[...]))
    gate = _mm(_sigmoid(lo[:, 128:256]), g2_ref[...])
    kk = k * kk_ref[...]
    kmod = k * (1.0 + (a - 1.0) * ka_ref[...])
    kk = kk * lax.rsqrt(jnp.maximum(_seg_sum(kk * kk), 1e-24))
    kb = kk * a

    NB = 4 * R
    lanehead = _iota((1, 256), 1) >> 6
    bi, bj = _iota((NB, NB), 0), _iota((NB, NB), 1)
    same = _ind((bi >> lc) == (bj >> lc))
    strict = same * _ind(bj < bi)
    incl = same * _ind(bj <= bi)
    eye_nb = _ind(bi == bj)
    si, sj = _iota((R, R), 0), _iota((R, R), 1)
    seg = _ind((si >> lc) == (sj >> lc))
    seg_ones = seg.astype(BF16)
    seg_tri = (seg * _ind(sj <= si)).astype(BF16)

    def bd(x):
        parts = []
        for s in range(G):
            xs = x[s * C:(s + 1) * C]
            for h in range(4):
                parts.append(jnp.where(lanehead == h, xs, 0.0))
        return jnp.concatenate(parts, axis=0)

    chains = [(sub, hh) for sub in range(NSUB) for hh in range(2)]
    ops = []
    for sub, hh in chains:
        sr, sl = slice(sub * R, (sub + 1) * R), slice(hh * 256, (hh + 1) * 256)
        lw4 = lw[sr, sl]
        clw = _mm_exact_lhs(seg_tri, lw4)
        clt = _mm_exact_lhs(seg_ones, lw4)
        e_out = jnp.exp(-clw)
        e_end = jnp.exp(clt - clw)
        b_rq = bd(r[sr, sl] * jnp.exp(clw))
        ops.append(dict(
            kq=bd(kk[sr, sl] * jnp.exp(clw - lw4)).astype(BF16), rq=b_rq, rq16=b_rq.astype(BF16),
            kk=bd(kmod[sr, sl] * e_out).astype(BF16), bk=bd(kb[sr, sl] * e_out).astype(BF16),
            kk2=bd(kmod[sr, sl] * e_end).astype(BF16), bk2=bd(kb[sr, sl] * e_end).astype(BF16),
            v=bd(v[sr, sl]).astype(BF16), p_end=jnp.exp(clt)))

    n_mat = [strict * _mm_nt(o['kq'], o['bk']) for o in ops]
    a_kk = [strict * _mm_nt(o['kq'], o['kk']) for o in ops]
    a_rb = [incl * _mm_nt(o['rq16'], o['bk']) for o in ops]
    a_rk = [incl * _mm_nt(o['rq16'], o['kk']) for o in ops]

    t_inv = [eye_nb - n for n in n_mat]
    pw = n_mat
    for _ in range(lc - 1):
        pw = [_mm(x, x) for x in pw]
        t_inv = [t + _mm(t, x) for t, x in zip(t_inv, pw)]

    av = [_mm(a, o['v']) for a, o in zip(a_kk, ops)]
    wu = [_mm(t, jnp.concatenate([o['kq'], x.astype(BF16)], axis=1)) for t, o, x in zip(t_inv, ops, av)]
    arb_wu = [_mm(a, x) for a, x in zip(a_rb, wu)]
    rq2 = [o['rq'] - x[:, 0:256] for o, x in zip(ops, arb_wu)]
    y_loc = [_mm(a, o['v']) - x[:, 256:512] for a, o, x in zip(a_rk, ops, arb_wu)]
    wu16 = [x.astype(BF16) for x in wu]

    upd = []
    for i, o in enumerate(ops):
        per_seg = []
        for s in range(G):
            rs = slice(s * 4 * C, (s + 1) * 4 * C)
            bwq = _mm_tn(o['bk2'][rs], wu16[i][rs, 0:256])
            add = _mm_tn(o['v'][rs], o['kk2'][rs]) - _mm_tn(wu16[i][rs, 256:512], o['bk2'][rs])
            per_seg.append((bwq.astype(BF16), add))
        upd.append(per_seg)

    y_parts = {}
    for i, (sub, hh) in enumerate(chains):
        ys = []
        for s in range(G):
            rs = slice(s * 4 * C, (s + 1) * 4 * C)
            slot = sub * G + s if has_state else 0
            g0 = hs_ref[slot, hh]
            g0b = g0.astype(BF16)
            yb = _mm_nt(rq2[i][rs], g0b) + y_loc[i][rs]
            y4 = yb[0:C]
            for h in range(1, 4):
                y4 = y4 + yb[h * C:(h + 1) * C]
            ys.append(y4)
            bwq, add = upd[i][s]
            hs_ref[slot, hh] = g0 * ops[i]['p_end'][s * C:s * C + 1, :] - _mm_nt(g0b, bwq) + add
        y_parts[(sub, hh)] = ys[0] if G == 1 else jnp.concatenate(ys, axis=0)
    y_subs = [jnp.concatenate([y_parts[(sub, 0)], y_parts[(sub, 1)]], axis=1) for sub in range(NSUB)]
    y = y_subs[0] if NSUB == 1 else jnp.concatenate(y_subs, axis=0)

    mean = _seg_sum(y) * (1.0 / 64.0)
    yc = y - mean
    var = _seg_sum(yc * yc) * (1.0 / 64.0)
    yn = yc * lax.rsqrt(var + RWKV_GN_EPS) * lnw_ref[...] + lnb_ref[...]
    bonus = _seg_sum(r * kmod * rk_ref[...]) * v
    y_ref[...] = ((yn + bonus) * gate).astype(y_ref.dtype)

    @pl.when(c == pl.num_programs(1) - 1)
    def _():
        fold = _ind((_iota((256, 64), 0) & 63) == _iota((256, 64), 1)).astype(BF16)
        for slot in range(n_slots):
            for hh in range(2):
                hout_ref[slot, 4 * hh:4 * hh + 4] = _mm_exact_rhs(hs_ref[slot, hh], fold).reshape(4, 64, 64)


def _row(n):
    return pl.BlockSpec((1, n), lambda *_: (0, 0))


def _full(shape):
    nd = len(shape)
    return pl.BlockSpec(shape, lambda *_: (0,) * nd)


def _any():
    return pl.BlockSpec(memory_space=pl.ANY)


def _winprep(w_in):
    depth, d, cols = w_in.shape
    tk = TILES['prep_tk']
    return pl.pallas_call(
        _winprep_kernel,
        grid=(depth, d // tk),
        in_specs=[pl.BlockSpec((None, tk, cols), lambda l, i: (l, i, 0))],
        out_specs=pl.BlockSpec((None, tk, PW), lambda l, i: (l, i, 0)),
        out_shape=jax.ShapeDtypeStruct((depth, d, PW), BF16),
        compiler_params=_cparams(("parallel", "parallel")),
        name="winprep",
    )(w_in)


def _inproj(x, g, w_all, l):
    t, tm, tn = x.shape[0], TILES['in_tm'], TILES['in_tn']
    return pl.pallas_call(
        _inproj_kernel,
        grid=(t // tm, PW // tn),
        in_specs=[pl.BlockSpec((tm, D_MODEL), lambda i, j: (i, 0)), _row(D_MODEL),
                  pl.BlockSpec((None, D_MODEL, tn), lambda i, j: (l, 0, j))],
        out_specs=pl.BlockSpec((tm, tn), lambda i, j: (i, j)),
        out_shape=jax.ShapeDtypeStruct((t, PW), F32),
        scratch_shapes=[pltpu.VMEM((tm, D_MODEL), BF16)],
        compiler_params=_cparams(("parallel", "arbitrary")),
        name="inproj",
    )(x, g, w_all)


def _outproj(ys, w_all, l, x):
    t, tm = x.shape[0], TILES['out_tm']
    yspec = pl.BlockSpec((tm, GW), lambda i: (i, 0))
    xspec = pl.BlockSpec((tm, D_MODEL), lambda i: (i, 0))
    return pl.pallas_call(
        _outproj_kernel,
        grid=(t // tm,),
        in_specs=[yspec, yspec, yspec, yspec,
                  pl.BlockSpec((None, D_MODEL, D_MODEL), lambda i: (l, 0, 0)), xspec],
        out_specs=xspec,
        out_shape=jax.ShapeDtypeStruct((t, D_MODEL), F32),
        compiler_params=_cparams(("parallel",)),
        name="outproj",
    )(*ys, w_all, x)


def _mlp(x, g, w1_all, w2_all, l, g_final, final):
    t, tm, tf = x.shape[0], TILES['mlp_tm'], TILES['mlp_tf']
    xspec = pl.BlockSpec((tm, D_MODEL), lambda i, k: (i, 0))
    return pl.pallas_call(
        functools.partial(_mlp_kernel, final=final),
        grid=(t // tm, D_FF // tf),
        in_specs=[xspec, _row(D_MODEL),
                  pl.BlockSpec((None, D_MODEL, tf), lambda i, k: (l, 0, k)),
                  pl.BlockSpec((None, tf, D_MODEL), lambda i, k: (l, k, 0)),
                  _row(D_MODEL)],
        out_specs=xspec,
        out_shape=jax.ShapeDtypeStruct((t, D_MODEL), F32),
        scratch_shapes=[pltpu.VMEM((tm, D_MODEL), BF16)],
        compiler_params=_cparams(("parallel", "arbitrary")),
        name="mlp",
    )(x, g, w1_all, w2_all, g_final)


def _mixer_call(kernel_fn, name, p, t_rows, n_seq, n_blk, rows, row0, col_specs, state_in, params, y_prev,
                state_out, scratch, tables=()):
    def rowmap(cb):
        return lambda s, c: (row0 + s * n_blk + c, cb)

    in_specs = [pl.BlockSpec((rows, w), rowmap(cb)) for w, cb in col_specs]
    args = [p] * len(col_specs)
    for arr, blk in state_in:
        nd = len(blk)
        in_specs.append(pl.BlockSpec(blk, lambda s, c, nd=nd: (s,) + (0,) * (nd - 1)))
        args.append(arr)
    for tab, blk in tables:
        in_specs.append(pl.BlockSpec(blk, lambda s, c: (c, 0)))
        args.append(tab)
    for arr in params:
        in_specs.append(_full(arr.shape))
        args.append(arr)
    aliases = {}
    if y_prev is not None:
        aliases = {len(args): 0}
        in_specs.append(_any())
        args.append(y_prev)
    out_specs = [pl.BlockSpec((rows, GW), lambda s, c: (row0 + s * n_blk + c, 0))]
    out_shape = [jax.ShapeDtypeStruct((t_rows, GW), BF16)]
    for shape, blk in state_out:
        nd = len(blk)
        out_specs.append(pl.BlockSpec(blk, lambda s, c, nd=nd: (s,) + (0,) * (nd - 1)))
        out_shape.append(jax.ShapeDtypeStruct(shape, F32))
    return pl.pallas_call(
        kernel_fn,
        grid=(n_seq, n_blk),
        in_specs=in_specs,
        out_specs=out_specs,
        out_shape=out_shape,
        scratch_shapes=scratch,
        input_output_aliases=aliases,
        compiler_params=_cparams(("arbitrary", "arbitrary")),
        name=name,
    )(*args)


def _run_ssm(p, dims, params, buf_s, h0_s):
    t, bp, lp, bs, ls = dims
    tp = bp * lp
    q = TILES['ssm_q']
    cols = [(SSM_XBC, CB_XBC), (GW, CB_Z), (128, CB_DT)]
    hshape = lambda n: [((n, SSM_HEADS, 64, SSM_STATE), (1, SSM_HEADS, 64, SSM_STATE))]
    scratch = lambda rows: [pltpu.VMEM((HIST + rows, SSM_XBC), F32), pltpu.VMEM((2, SSM_STATE, 256), F32)]
    y, h_p = _mixer_call(functools.partial(_ssm_kernel, Q=q, has_state=False), "ssm_prompt", p, t, bp, lp // q, q,
                         0, cols, [], params, None, hshape(bp), scratch(q))
    y, h_s = _mixer_call(functools.partial(_ssm_kernel, Q=ls, has_state=True), "ssm_sample", p, t, bs, 1, ls,
                         tp // ls, cols, [(buf_s, (1, HIST, SSM_XBC)), (h0_s, (1, SSM_HEADS, 64, SSM_STATE))],
                         params, y, hshape(bs), scratch(ls))
    return y, h_p, h_s


def _run_lru(p, dims, params, buf_s, h0_s):
    t, bp, lp, bs, ls = dims
    tp = bp * lp
    r = TILES['lru_r']
    cols = [(2 * GW, CB_LRU)]
    hshape = lambda n: [((n, 1, GW), (1, 1, GW))]
    scratch = lambda rows: [pltpu.VMEM((HIST + rows, GW), F32), pltpu.VMEM((1, GW), F32)]
    y, h_p = _mixer_call(functools.partial(_lru_kernel, R=r, has_state=False), "lru_prompt", p, t, bp, lp // r, r, 0,
                         cols, [], params, None, hshape(bp), scratch(r))
    y, h_s = _mixer_call(functools.partial(_lru_kernel, R=ls, has_state=True), "lru_sample", p, t, bs, 1, ls,
                         tp // ls, cols, [(buf_s, (1, HIST, GW)), (h0_s, (1, 1, GW))], params, y, hshape(bs),
                         scratch(ls))
    return y, h_p[:, 0], h_s[:, 0]


def _run_swa(p, dims, rope_p, rope_s, sinks, k0_s, v0_s):
    t, bp, lp, bs, ls = dims
    tp = bp * lp
    q, ns = TILES['swa_q'], TILES['swa_ns']
    cols = [(GW, CB_Q), (256, CB_KV)]
    cshape = lambda n, g: [((n, WINDOW, 128), (g, WINDOW, 128))] * 2
    scratch = lambda g: [pltpu.VMEM((g, WINDOW, 128), F32), pltpu.VMEM((g, WINDOW, 128), F32)]
    y, k_p, v_p = _mixer_call(functools.partial(_swa_kernel, QB=q, NS=1, has_state=False), "swa_prompt", p, t, bp,
                              lp // q, q, 0, cols, [], [sinks], None, cshape(bp, 1), scratch(1),
                              tables=[(tab, (q, GW)) for tab in rope_p])
    y, k_s, v_s = _mixer_call(functools.partial(_swa_kernel, QB=ls, NS=ns, has_state=True), "swa_sample", p, t,
                              bs // ns, 1, ns * ls, tp // (ns * ls), cols,
                              [(k0_s, (ns, WINDOW, 128)), (v0_s, (ns, WINDOW, 128))], [sinks], y, cshape(bs, ns),
                              scratch(ns), tables=[(tab, (ls, GW)) for tab in rope_s])
    return y, k_p, v_p, k_s, v_s


def _run_rwkv(p, dims, params, sh_rows_s, s0_s):
    t, bp, lp, bs, ls = dims
    tp = bp * lp
    c, nsub = TILES['rwkv_c'], TILES['rwkv_sub']
    g, nsub_s = TILES['rwkv_g'], TILES['rwkv_sub_s']
    cols = [(GW, CB_R), (GW, CB_K), (GW, CB_V), (256, CB_LORA)]
    sshape = lambda n, blk: [((n, 8, 64, 64), (blk, 8, 64, 64))]
    scratch = lambda slots: [pltpu.VMEM((slots, 2, 256, 256), F32), pltpu.VMEM((HIST, RWKV_COLS), F32)]
    rows_p = nsub * c
    y, s_p = _mixer_call(functools.partial(_rwkv_kernel, NSUB=nsub, G=1, C=c, has_state=False), "rwkv_prompt", p, t,
                         bp, lp // rows_p, rows_p, 0, cols, [], params, None, sshape(bp, 1), scratch(1))
    nseq = nsub_s * g
    rows_s = nseq * ls
    y, s_s = _mixer_call(functools.partial(_rwkv_kernel, NSUB=nsub_s, G=g, C=ls, has_state=True), "rwkv_sample", p, t,
                         bs // nseq, 1, rows_s, tp // rows_s, cols,
                         [(sh_rows_s, (rows_s, RWKV_COLS)), (s0_s, (nseq, 8, 64, 64))], params, y, sshape(bs, nseq),
                         scratch(nseq))
    return y, s_p, s_s


def _rope_tables(pos):
    half = ROT_DIM // 2
    inv = ROPE_THETA ** (-(jnp.arange(half, dtype=F32) * 2.0) / ROT_DIM)
    ang = pos.astype(F32)[:, None] * inv
    cos, sin = jnp.cos(ang), jnp.sin(ang)
    n = pos.shape[0]
    zeros = jnp.zeros((n, half), F32)
    rest = SWA_HEADDIM - ROT_DIM
    cos64 = jnp.concatenate([cos, cos, jnp.ones((n, rest), F32)], axis=1)
    sa64 = jnp.concatenate([-sin, zeros, jnp.zeros((n, rest), F32)], axis=1)
    sb64 = jnp.concatenate([zeros, sin, jnp.zeros((n, rest), F32)], axis=1)
    return tuple(jnp.tile(t, (1, GW // SWA_HEADDIM)) for t in (cos64, sa64, sb64))


def _pad_rows_front(x, axis, total):
    pad = [(0, 0)] * x.ndim
    pad[axis] = (total - x.shape[axis], 0)
    return jnp.pad(x, pad)


def kernel(x_prompt, x_sample, state_ssm, state_ssm_conv, state_rwkv, state_rwkv_shift, cache_swa_k,
           cache_swa_v, state_lru, state_lru_conv, norm_mix, w_in, ssm_conv_w, ssm_conv_b, ssm_dt_bias,
           ssm_a_log, ssm_d, ssm_norm, rwkv_mu, rwkv_w0, rwkv_w2, rwkv_a0, rwkv_a2, rwkv_g2, rwkv_k_k,
           rwkv_k_a, rwkv_r_k, rwkv_ln_w, rwkv_ln_b, swa_sinks, lru_conv_w, lru_conv_b, lru_wa, lru_ba,
           lru_wi, lru_bi, lru_lambda, w_out, norm_mlp, mlp_w1, mlp_w2, norm_final):
    bp, lp, _ = x_prompt.shape
    bs, ls, _ = x_sample.shape
    tp, ts = bp * lp, bs * ls
    t = tp + ts
    dims = (t, bp, lp, bs, ls)
    depth = w_in.shape[0]

    w_in_r = _winprep(w_in)
    w_out_b = w_out.astype(BF16)
    w1_b = mlp_w1.astype(BF16)
    w2_b = mlp_w2.astype(BF16)
    eye8 = jnp.eye(8, dtype=F32)
    lru_wa_d = jnp.einsum('lncd,nm->lncmd', lru_wa, eye8).reshape(depth, GW, GW).astype(BF16)
    lru_wi_d = jnp.einsum('lncd,nm->lncmd', lru_wi, eye8).reshape(depth, GW, GW).astype(BF16)
    z64 = jnp.zeros((depth, 64, GW), F32)
    rwkv_w2_p = jnp.concatenate([rwkv_w2, z64], axis=1).astype(BF16)
    rwkv_a2_p = jnp.concatenate([z64, rwkv_a2], axis=1).astype(BF16)
    rwkv_g2_b = rwkv_g2.astype(BF16)
    dtb_p = jnp.pad(ssm_dt_bias, ((0, 0), (0, 128 - SSM_HEADS)))
    alog_p = jnp.pad(ssm_a_log, ((0, 0), (0, 128 - SSM_HEADS)))
    d_e = jnp.repeat(ssm_d, 64, axis=-1)
    sinks_b = jnp.broadcast_to(swa_sinks[:, :, None], (depth, 8, 128))
    rope_p = _rope_tables(jnp.arange(lp))
    rope_s = _rope_tables(PAST_LEN + jnp.arange(ls))

    def r2(a, l):
        return a[l][None, :]

    ssm_buf = _pad_rows_front(state_ssm_conv, 2, HIST)
    lru_buf = _pad_rows_front(state_lru_conv, 2, HIST)
    lru_h0 = state_lru[:, :, None, :]
    sh_rows = jnp.pad(state_rwkv_shift[:, :, None, :], ((0, 0), (0, 0), (0, ls - 1), (0, 0))).reshape(depth, ts, RWKV_COLS)
    swa_k0 = cache_swa_k.reshape(depth, bs, WINDOW, 128)
    swa_v0 = cache_swa_v.reshape(depth, bs, WINDOW, 128)

    x = jnp.concatenate([x_prompt.reshape(tp, D_MODEL), x_sample.reshape(ts, D_MODEL)], axis=0)

    names = ('ssm_p', 'ssm_s', 'conv_p', 'conv_s', 'rwkv_p', 'rwkv_s', 'shift_p', 'shift_s',
             'k_p', 'k_s', 'v_p', 'v_s', 'lru_p', 'lru_s', 'lconv_p', 'lconv_s')
    outs = {n: [] for n in names}
    for l in range(depth):
        p = _inproj(x, r2(norm_mix, l), w_in_r, l)

        ssm_params = [ssm_conv_w[l], r2(ssm_conv_b, l), r2(dtb_p, l), r2(alog_p, l), r2(d_e, l), r2(ssm_norm, l)]
        y_ssm, hs_p, hs_s = _run_ssm(p, dims, ssm_params, ssm_buf[l], state_ssm[l])
        rwkv_params = [r2(rwkv_mu, l), r2(rwkv_w0, l), rwkv_w2_p[l], r2(rwkv_a0, l), rwkv_a2_p[l], rwkv_g2_b[l],
                       r2(rwkv_k_k, l), r2(rwkv_k_a, l), rwkv_r_k[l].reshape(1, GW), r2(rwkv_ln_w, l),
                       r2(rwkv_ln_b, l)]
        y_rwkv, s_p, s_s = _run_rwkv(p, dims, rwkv_params, sh_rows[l], state_rwkv[l])
        y_swa, k_p, v_p, k_s, v_s = _run_swa(p, dims, rope_p, rope_s, sinks_b[l], swa_k0[l], swa_v0[l])
        lru_params = [lru_conv_w[l], r2(lru_conv_b, l), lru_wa_d[l], r2(lru_ba, l), lru_wi_d[l], r2(lru_bi, l),
                      r2(lru_lambda, l)]
        y_lru, lh_p, lh_s = _run_lru(p, dims, lru_params, lru_buf[l], lru_h0[l])

        x1 = _outproj([y_ssm, y_rwkv, y_swa, y_lru], w_out_b, l, x)
        x = _mlp(x1, r2(norm_mlp, l), w1_b, w2_b, l, norm_final[None, :], l + 1 == depth)

        def tail_p(n, c0, c1):
            return jnp.stack([p[b * lp + lp - n:b * lp + lp, c0:c1] for b in range(bp)], axis=0)

        ps = p[tp:].reshape(bs, ls, PW)
        lora0 = CB_LORA * 256
        outs['ssm_p'].append(hs_p)
        outs['ssm_s'].append(hs_s)
        outs['conv_p'].append(tail_p(3, 0, SSM_XBC))
        outs['conv_s'].append(ps[:, ls - 3:, 0:SSM_XBC])
        outs['rwkv_p'].append(s_p)
        outs['rwkv_s'].append(s_s)
        outs['shift_p'].append(jnp.concatenate([tail_p(1, 5 * GW, 8 * GW), tail_p(1, lora0, lora0 + 256)], axis=-1)[:, 0])
        outs['shift_s'].append(jnp.concatenate([ps[:, ls - 1, 5 * GW:8 * GW], ps[:, ls - 1, lora0:lora0 + 256]], axis=-1))
        outs['k_p'].append(k_p.reshape(bp, WINDOW, 2, SWA_HEADDIM))
        outs['k_s'].append(k_s.reshape(bs, WINDOW, 2, SWA_HEADDIM))
        outs['v_p'].append(v_p.reshape(bp, WINDOW, 2, SWA_HEADDIM))
        outs['v_s'].append(v_s.reshape(bs, WINDOW, 2, SWA_HEADDIM))
        outs['lru_p'].append(lh_p)
        outs['lru_s'].append(lh_s)
        outs['lconv_p'].append(tail_p(3, SSM_XBC, SSM_XBC + GW))
        outs['lconv_s'].append(ps[:, ls - 3:, SSM_XBC:SSM_XBC + GW])

    st = {n: jnp.stack(v, axis=0) for n, v in outs.items()}
    return (x[:tp].reshape(bp, lp, D_MODEL), x[tp:].reshape(bs, ls, D_MODEL),
            st['ssm_p'], st['ssm_s'], st['conv_p'], st['conv_s'], st['rwkv_p'], st['rwkv_s'],
            st['shift_p'], st['shift_s'], st['k_p'], st['k_s'], st['v_p'], st['v_s'],
            st['lru_p'], st['lru_s'], st['lconv_p'], st['lconv_s'])
```

```python
import functools
import math

import jax
import jax.numpy as jnp
from jax import lax
from jax.experimental import pallas as pl
from jax.experimental.pallas import tpu as pltpu

F32 = jnp.float32
BF16 = jnp.bfloat16

D_MODEL = 2048
GW = 512
RMS_EPS = 1e-6
CONV_W = 4
SSM_HEADS = 8
SSM_STATE = 128
SSM_XBC = 1024
RWKV_COLS = 1792
RWKV_GN_EPS = 64e-5
SWA_HEADDIM = 64
WINDOW = 128
ROT_DIM = 16
ROPE_THETA = 500000.0
LRU_C = 8.0
D_FF = 4 * D_MODEL
PAST_LEN = 16384

PW = 5376
CB_XBC, CB_LRU = 0, 1
CB_Z, CB_R, CB_K, CB_V, CB_Q = 4, 5, 6, 7, 8
CB_KV, CB_LORA = 18, 19
CB_DT = 40

HIST = 8
VMEM_LIMIT = 56 * 1024 * 1024

TILES = dict(
    prep_tk=256, in_tm=1408, in_tn=768, out_tm=704, mlp_tm=704, mlp_tf=1024,
    ssm_q=128, swa_q=128, lru_r=256, rwkv_c=64, rwkv_sub=4,
    swa_ns=4, rwkv_g=4, rwkv_sub_s=2,
)


def _cparams(sem):
    return pltpu.CompilerParams(dimension_semantics=sem, vmem_limit_bytes=VMEM_LIMIT)


def _mm(a, b):
    return jnp.dot(a.astype(BF16), b.astype(BF16), preferred_element_type=F32)


def _mm_nt(a, b):
    return lax.dot_general(a.astype(BF16), b.astype(BF16), (((1,), (1,)), ((), ())),
                           preferred_element_type=F32)


def _mm_tn(a, b):
    return lax.dot_general(a.astype(BF16), b.astype(BF16), (((0,), (0,)), ((), ())),
                           preferred_element_type=F32)


def _split3(x):
    x1 = x.astype(BF16)
    r = x - x1.astype(F32)
    x2 = r.astype(BF16)
    r = r - x2.astype(F32)
    return x1, x2, r.astype(BF16)


def _mm_exact_rhs(a, sel):
    p1, p2, p3 = _split3(a)
    return _mm(p1, sel) + _mm(p2, sel) + _mm(p3, sel)


def _mm_split2_rhs(a, sel):
    p1 = a.astype(BF16)
    p2 = (a - p1.astype(F32)).astype(BF16)
    return _mm(p1, sel) + _mm(p2, sel)


def _mm_exact_lhs(sel, b):
    p1, p2, p3 = _split3(b)
    return _mm(sel, p1) + _mm(sel, p2) + _mm(sel, p3)


def _mm_nt_exact_lhs(sel, b):
    p1, p2, p3 = _split3(b)
    return _mm_nt(sel, p1) + _mm_nt(sel, p2) + _mm_nt(sel, p3)


def _softplus(x):
    return jnp.maximum(x, 0.0) + jnp.log1p(jnp.exp(-jnp.abs(x)))


def _sigmoid(x):
    return 1.0 / (1.0 + jnp.exp(-x))


def _iota(shape, dim):
    return lax.broadcasted_iota(jnp.int32, shape, dim)


def _ind(mask):
    return jnp.where(mask, 1.0, 0.0)


def _rms(x, g):
    ms = jnp.mean(x * x, axis=-1, keepdims=True)
    return x * lax.rsqrt(ms + RMS_EPS) * g


def _conv_from_hist(ext_ref, rows, cw_ref, cb_ref):
    acc = cb_ref[...] + cw_ref[0:1, :] * ext_ref[HIST - 3:HIST - 3 + rows, :]
    for j in range(1, CONV_W):
        acc = acc + cw_ref[j:j + 1, :] * ext_ref[HIST - 3 + j:HIST - 3 + j + rows, :]
    return acc


def _winprep_kernel(w_ref, o_ref):
    o_rwkv = GW + SSM_XBC + SSM_HEADS
    o_swa = o_rwkv + RWKV_COLS
    o_lru = o_swa + GW + 256
    sections = ((GW, SSM_XBC), (o_lru, 2 * GW), (0, GW), (o_rwkv, 3 * GW), (o_swa, GW), (o_swa + GW, 256),
                (o_rwkv + 3 * GW, 256), (GW + SSM_XBC, SSM_HEADS))
    dst = 0
    for src, width in sections:
        o_ref[:, dst:dst + width] = w_ref[:, src:src + width]
        dst += width
    o_ref[:, dst:PW] = jnp.zeros((o_ref.shape[0], PW - dst), BF16)


def _inproj_kernel(x_ref, g_ref, w_ref, o_ref, h_ref):
    @pl.when(pl.program_id(1) == 0)
    def _():
        h_ref[...] = _rms(x_ref[...], g_ref[...]).astype(BF16)

    o_ref[...] = jnp.dot(h_ref[...], w_ref[...], preferred_element_type=F32)


def _outproj_kernel(y0_ref, y1_ref, y2_ref, y3_ref, w_ref, x_ref, x1_ref):
    acc = x_ref[...]
    for i, y_ref in enumerate((y0_ref, y1_ref, y2_ref, y3_ref)):
        acc = acc + jnp.dot(y_ref[...], w_ref[i * GW:(i + 1) * GW, :], preferred_element_type=F32)
    x1_ref[...] = acc


def _mlp_kernel(x_ref, g_ref, w1_ref, w2_ref, gf_ref, xo_ref, h_ref, *, final):
    k = pl.program_id(1)

    @pl.when(k == 0)
    def _():
        x = x_ref[...]
        h_ref[...] = _rms(x, g_ref[...]).astype(BF16)
        xo_ref[...] = x

    a = jnp.dot(h_ref[...], w1_ref[...], preferred_element_type=F32)
    a = jnp.square(jnp.maximum(a, 0.0)).astype(BF16)
    xo_ref[...] += jnp.dot(a, w2_ref[...], preferred_element_type=F32)

    if final:
        @pl.when(k == pl.num_programs(1) - 1)
        def _():
            xo_ref[...] = _rms(xo_ref[...], gf_ref[...])


def _ssm_kernel(*refs, Q, has_state):
    if has_state:
        (xbc_ref, z_ref, dt_ref, buf_ref, h0_ref, cw_ref, cb_ref, dtb_ref, alog_ref, de_ref, nw_ref,
         _, y_ref, hout_ref, ext_ref, h_ref) = refs
    else:
        (xbc_ref, z_ref, dt_ref, cw_ref, cb_ref, dtb_ref, alog_ref, de_ref, nw_ref,
         y_ref, hout_ref, ext_ref, h_ref) = refs
    c = pl.program_id(1)

    @pl.when(c == 0)
    def _():
        if has_state:
            ext_ref[0:HIST, :] = buf_ref[0]
            for g in range(2):
                h_ref[g] = h0_ref[0, 4 * g:4 * g + 4].reshape(256, SSM_STATE).T
        else:
            ext_ref[0:HIST, :] = jnp.zeros((HIST, SSM_XBC), F32)
            h_ref[...] = jnp.zeros(h_ref.shape, F32)

    ext_ref[HIST:HIST + Q, :] = xbc_ref[...]
    pre = _conv_from_hist(ext_ref, Q, cw_ref, cb_ref)
    ext_ref[0:HIST, :] = ext_ref[Q:Q + HIST, :]
    xbc = pre * _sigmoid(pre)
    x = xbc[:, 0:GW]
    bm = xbc[:, GW:GW + 256]
    cm = xbc[:, GW + 256:GW + 512]

    dt = _softplus(dt_ref[...] + dtb_ref[...])
    a_neg = jnp.where(_iota((1, 128), 1) < SSM_HEADS, -jnp.exp(alog_ref[...]), 0.0)
    tri = (_iota((Q, Q), 0) >= _iota((Q, Q), 1))
    cum = _mm_exact_lhs(_ind(tri).astype(BF16), dt * a_neg)
    expand = _ind(_iota((128, GW), 0) == (_iota((128, GW), 1) >> 6)).astype(BF16)
    dt_e = _mm_exact_rhs(dt, expand)
    cum_e = _mm_exact_rhs(cum, expand)
    sel = _ind(_iota((SSM_HEADS, 128), 0) == _iota((SSM_HEADS, 128), 1)).astype(BF16)
    cum_t = _mm_nt_exact_lhs(sel, cum)
    last = cum_e[Q - 1:Q, :]
    expc_e = jnp.exp(cum_e)
    tail_e = jnp.exp(last - cum_e)
    dtot_e = jnp.exp(last)
    dx = dt_e * x
    lanehead = _iota((1, 256), 1) >> 6

    ys = []
    for g in range(2):
        bg = bm[:, g * 128:(g + 1) * 128]
        cg = cm[:, g * 128:(g + 1) * 128]
        cb = _mm_nt(cg, bg)
        h_t = h_ref[g]
        yg = _mm(cg, h_t) * expc_e[:, g * 256:(g + 1) * 256]
        dxg = dx[:, g * 256:(g + 1) * 256]
        for hh in range(4):
            h = 4 * g + hh
            decay = jnp.exp(jnp.minimum(cum[:, h:h + 1] - cum_t[h:h + 1, :], 0.0))
            m = jnp.where(tri, cb * decay, 0.0)
            yg = yg + _mm(m, jnp.where(lanehead == hh, dxg, 0.0))
        ys.append(yg)
        h_ref[g] = (h_t * dtot_e[:, g * 256:(g + 1) * 256]
                    + _mm_tn(bg, dxg * tail_e[:, g * 256:(g + 1) * 256]))

    y = jnp.concatenate(ys, axis=1) + de_ref[...] * x
    z = z_ref[...]
    y = y * (z * _sigmoid(z))
    outs = []
    for g in range(2):
        yg = y[:, g * 256:(g + 1) * 256]
        outs.append(yg * lax.rsqrt(jnp.mean(yg * yg, axis=-1, keepdims=True) + RMS_EPS))
    y_ref[...] = (jnp.concatenate(outs, axis=1) * nw_ref[...]).astype(y_ref.dtype)

    @pl.when(c == pl.num_programs(1) - 1)
    def _():
        for g in range(2):
            hout_ref[0, 4 * g:4 * g + 4] = h_ref[g].T.reshape(4, 64, SSM_STATE)


def _lru_kernel(*refs, R, has_state):
    if has_state:
        (u_ref, buf_ref, h0_ref, cw_ref, cb_ref, wa_ref, ba_ref, wi_ref, bi_ref, lam_ref,
         _, y_ref, hout_ref, ext_ref, h_ref) = refs
    else:
        (u_ref, cw_ref, cb_ref, wa_ref, ba_ref, wi_ref, bi_ref, lam_ref,
         y_ref, hout_ref, ext_ref, h_ref) = refs
    c = pl.program_id(1)

    @pl.when(c == 0)
    def _():
        if has_state:
            ext_ref[0:HIST, :] = buf_ref[0]
            h_ref[...] = h0_ref[0]
        else:
            ext_ref[0:HIST, :] = jnp.zeros((HIST, GW), F32)
            h_ref[...] = jnp.zeros(h_ref.shape, F32)

    u = u_ref[...]
    gate = u[:, GW:]
    ext_ref[HIST:HIST + R, :] = u[:, :GW]
    xc = _conv_from_hist(ext_ref, R, cw_ref, cb_ref)
    ext_ref[0:HIST, :] = ext_ref[R:R + HIST, :]

    rg = _sigmoid(_mm(xc, wa_ref[...]) + ba_ref[...])
    ig = _sigmoid(_mm(xc, wi_ref[...]) + bi_ref[...])
    log_a = -LRU_C * rg * _softplus(-lam_ref[...])
    a = jnp.exp(log_a)
    th = jnp.tanh(log_a)
    b = jnp.sqrt(-2.0 * th / (1.0 - th)) * (ig * xc)

    row = _iota((R, 1), 0)
    s = 1
    while s < R:
        keep = row >= s
        a_prev = jnp.where(keep, pltpu.roll(a, s, 0), 1.0)
        b_prev = jnp.where(keep, pltpu.roll(b, s, 0), 0.0)
        b = b + a * b_prev
        a = a * a_prev
        s *= 2
    h = b + a * h_ref[...]
    h_ref[...] = h[R - 1:R, :]
    hout_ref[0] = h[R - 1:R, :]

    gelu = 0.5 * gate * (1.0 + jnp.tanh(math.sqrt(2.0 / math.pi) * (gate + 0.044715 * (gate * gate * gate))))
    y_ref[...] = (h * gelu).astype(y_ref.dtype)


def _swa_kernel(*refs, QB, NS, has_state):
    if has_state:
        (q_ref, kv_ref, kc_ref, vc_ref, cos_ref, sa_ref, sb_ref, sink_ref,
         _, y_ref, kout_ref, vout_ref, hk_ref, hv_ref) = refs
    else:
        (q_ref, kv_ref, cos_ref, sa_ref, sb_ref, sink_ref,
         y_ref, kout_ref, vout_ref, hk_ref, hv_ref) = refs
    c = pl.program_id(1)
    nk = WINDOW + QB
    lq = QB.bit_length() - 1

    @pl.when(c == 0)
    def _():
        if has_state:
            hk_ref[...] = kc_ref[...]
            hv_ref[...] = vc_ref[...]
        else:
            hk_ref[...] = jnp.zeros(hk_ref.shape, F32)
            hv_ref[...] = jnp.zeros(hv_ref.shape, F32)

    def rope(x, width):
        cos, sa, sb = cos_ref[:, 0:width], sa_ref[:, 0:width], sb_ref[:, 0:width]
        return x * cos + pltpu.roll(x, width - ROT_DIM // 2, 1) * sa + pltpu.roll(x, ROT_DIM // 2, 1) * sb

    kj = _iota((nk, 4 * QB), 0)
    qi = _iota((nk, 4 * QB), 1) & (QB - 1)
    lo = qi if has_state else jnp.where(c == 0, jnp.maximum(qi, WINDOW), qi)
    neg_mask = jnp.where(kj >= lo, jnp.where(kj <= qi + WINDOW, 0.0, -jnp.inf), -jnp.inf)
    upper = _iota((1, 128), 1) >= SWA_HEADDIM
    lower = _iota((1, 128), 1) < SWA_HEADDIM
    chead = _iota((1, 4 * QB), 1) >> lq
    sink_rows = []
    for g in range(2):
        sr = sink_ref[4 * g + 3:4 * g + 4, 0:1]
        for hh in (2, 1, 0):
            sr = jnp.where(chead == hh, sink_ref[4 * g + hh:4 * g + hh + 1, 0:1], sr)
        sink_rows.append(sr)

    for s in range(NS):
        rows = slice(s * QB, (s + 1) * QB)
        q = rope(q_ref[rows, :], GW) * (SWA_HEADDIM ** -0.5)
        kv = kv_ref[rows, :]
        k_new = rope(kv[:, 0:128], 128)
        v_new = kv[:, 128:256]
        keys = jnp.concatenate([hk_ref[s], k_new], axis=0)
        vals = jnp.concatenate([hv_ref[s], v_new], axis=0)

        tiles = [None] * 4
        for g in range(2):
            own = upper if g == 1 else lower
            kd = jnp.where(own, keys, pltpu.roll(keys, SWA_HEADDIM, 1))
            vd = jnp.where(own, vals, pltpu.roll(vals, SWA_HEADDIM, 1))
            parts = []
            for hh in range(4):
                h = 4 * g + hh
                mine = upper if h % 2 == 1 else lower
                parts.append(jnp.where(mine, q[:, 128 * (h // 2):128 * (h // 2 + 1)], 0.0))
            sc = _mm_nt(kd, jnp.concatenate(parts, axis=0)) + neg_mask
            m = jnp.maximum(jnp.max(sc, axis=0, keepdims=True), sink_rows[g])
            e = jnp.exp(sc - m)
            den = jnp.sum(e, axis=0, keepdims=True) + jnp.exp(sink_rows[g] - m)
            o = _mm_tn(e / den, vd)
            for hh in range(4):
                h = 4 * g + hh
                oh = o[hh * QB:(hh + 1) * QB]
                tiles[h // 2] = oh if h % 2 == 0 else jnp.where(upper, oh, tiles[h // 2])
        y_ref[rows, :] = jnp.concatenate(tiles, axis=1).astype(y_ref.dtype)

        if QB == WINDOW:
            hk_ref[s] = k_new
            hv_ref[s] = v_new
        else:
            hk_ref[s] = keys[QB:, :]
            hv_ref[s] = vals[QB:, :]
    kout_ref[...] = hk_ref[...]
    vout_ref[...] = hv_ref[...]


def _seg_sum(x):
    ones = _ind((_iota((256, 256), 0) >> 6) == (_iota((256, 256), 1) >> 6)).astype(BF16)
    return jnp.concatenate([_mm_split2_rhs(x[:, 0:256], ones), _mm_split2_rhs(x[:, 256:512], ones)], axis=1)


def _rwkv_kernel(*refs, NSUB, G, C, has_state):
    if has_state:
        (r_ref, k_ref, v_ref, lo_ref, sh_ref, h0_ref, mu_ref, w0_ref, w2_ref, a0_ref, a2_ref, g2_ref,
         kk_ref, ka_ref, rk_ref, lnw_ref, lnb_ref, _, y_ref, hout_ref, hs_ref, carry_ref) = refs
    else:
        (r_ref, k_ref, v_ref, lo_ref, mu_ref, w0_ref, w2_ref, a0_ref, a2_ref, g2_ref,
         kk_ref, ka_ref, rk_ref, lnw_ref, lnb_ref, y_ref, hout_ref, hs_ref, carry_ref) = refs
    c = pl.program_id(1)
    R = G * C
    RB = NSUB * R
    lc = C.bit_length() - 1
    n_slots = NSUB * G if has_state else 1

    @pl.when(c == 0)
    def _():
        if has_state:
            for slot in range(n_slots):
                for hh in range(2):
                    rows = [jnp.pad(h0_ref[slot, 4 * hh + h], ((0, 0), (64 * h, 192 - 64 * h))) for h in range(4)]
                    hs_ref[slot, hh] = jnp.concatenate(rows, axis=0)
        else:
            hs_ref[...] = jnp.zeros(hs_ref.shape, F32)
            carry_ref[...] = jnp.zeros(carry_ref.shape, F32)

    row = _iota((RB, 1), 0)
    first = ((row & (C - 1)) == 0) if has_state else (row == 0)
    mixed = []
    for ref, off, width in ((r_ref, 0, GW), (k_ref, GW, GW), (v_ref, 2 * GW, GW), (lo_ref, 3 * GW, 256)):
        pf = ref[...]
        if has_state:
            fv = sh_ref[:, off:off + width]
        else:
            fv = carry_ref[0:1, off:off + width]
        shifted = jnp.where(first, fv, pltpu.roll(pf, 1, 0))
        mixed.append(pf + (shifted - pf) * mu_ref[:, off:off + width])
        if not has_state:
            carry_ref[0:1, off:off + width] = pf[RB - 1:RB, :]
    r, k, v, lo = mixed

    lo_wa = lo[:, 0:128]
    w_log = -_softplus(-(w0_ref[...] + _mm(jnp.tanh(lo_wa), w2_ref[...]))) - 0.5
    lw = -jnp.exp(w_log)
    a = _sigmoid(a0_ref[...] + _mm(lo_wa, a2_ref[...]))
    gate = _mm(_sigmoid(lo[:, 128:256]), g2_ref[...])
    kk = k * kk_ref[...]
    kmod = k * (1.0 + (a - 1.0) * ka_ref[...])
    kk = kk * lax.rsqrt(jnp.maximum(_seg_sum(kk * kk), 1e-24))
    kb = kk * a

    NB = 4 * R
    lanehead = _iota((1, 256), 1) >> 6
    bi, bj = _iota((NB, NB), 0), _iota((NB, NB), 1)
    same = _ind((bi >> lc) == (bj >> lc))
    strict = same * _ind(bj < bi)
    incl = same * _ind(bj <= bi)
    eye_nb = _ind(bi == bj)
    si, sj = _iota((R, R), 0), _iota((R, R), 1)
    seg = _ind((si >> lc) == (sj >> lc))
    seg_ones = seg.astype(BF16)
    seg_tri = (seg * _ind(sj <= si)).astype(BF16)

    def bd(x):
        parts = []
        for s in range(G):
            xs = x[s * C:(s + 1) * C]
            for h in range(4):
                parts.append(jnp.where(lanehead == h, xs, 0.0))
        return jnp.concatenate(parts, axis=0)

    chains = [(sub, hh) for sub in range(NSUB) for hh in range(2)]
    ops = []
    for sub, hh in chains:
        sr, sl = slice(sub * R, (sub + 1) * R), slice(hh * 256, (hh + 1) * 256)
        lw4 = lw[sr, sl]
        clw = _mm_exact_lhs(seg_tri, lw4)
        clt = clw[C - 1:C, :] if G == 1 else _mm_exact_lhs(seg_ones, lw4)
        e_out = jnp.exp(-clw)
        e_end = jnp.exp(clt - clw)
        b_rq = bd(r[sr, sl] * jnp.exp(clw))
        ops.append(dict(
            kq=bd(kk[sr, sl] * jnp.exp(clw - lw4)).astype(BF16), rq=b_rq, rq16=b_rq.astype(BF16),
            kk=bd(kmod[sr, sl] * e_out).astype(BF16), bk=bd(kb[sr, sl] * e_out).astype(BF16),
            kk2=bd(kmod[sr, sl] * e_end).astype(BF16), bk2=bd(kb[sr, sl] * e_end).astype(BF16),
            v=bd(v[sr, sl]).astype(BF16), p_end=jnp.exp(clt)))

    n_mat = [strict * _mm_nt(o['kq'], o['bk']) for o in ops]
    a_kk = [strict * _mm_nt(o['kq'], o['kk']) for o in ops]
    a_rb = [incl * _mm_nt(o['rq16'], o['bk']) for o in ops]
    a_rk = [incl * _mm_nt(o['rq16'], o['kk']) for o in ops]

    t_inv = [eye_nb - n for n in n_mat]
    pw = n_mat
    for _ in range(lc - 1):
        pw = [_mm(x, x) for x in pw]
        t_inv = [t + _mm(t, x) for t, x in zip(t_inv, pw)]

    av = [_mm(a, o['v']) for a, o in zip(a_kk, ops)]
    wu = [_mm(t, jnp.concatenate([o['kq'], x.astype(BF16)], axis=1)) for t, o, x in zip(t_inv, ops, av)]
    arb_wu = [_mm(a, x) for a, x in zip(a_rb, wu)]
    rq2 = [o['rq'] - x[:, 0:256] for o, x in zip(ops, arb_wu)]
    y_loc = [_mm(a, o['v']) - x[:, 256:512] for a, o, x in zip(a_rk, ops, arb_wu)]
    wu16 = [x.astype(BF16) for x in wu]

    upd = []
    for i, o in enumerate(ops):
        per_seg = []
        for s in range(G):
            rs = slice(s * 4 * C, (s + 1) * 4 * C)
            bwq = _mm_tn(o['bk2'][rs], wu16[i][rs, 0:256])
            add = _mm_tn(o['v'][rs], o['kk2'][rs]) - _mm_tn(wu16[i][rs, 256:512], o['bk2'][rs])
            per_seg.append((bwq.astype(BF16), add))
        upd.append(per_seg)

    y_parts = {}
    for i, (sub, hh) in enumerate(chains):
        ys = []
        for s in range(G):
            rs = slice(s * 4 * C, (s + 1) * 4 * C)
            slot = sub * G + s if has_state else 0
            g0 = hs_ref[slot, hh]
            g0b = g0.astype(BF16)
            yb = _mm_nt(rq2[i][rs], g0b) + y_loc[i][rs]
            y4 = yb[0:C]
            for h in range(1, 4):
                y4 = y4 + yb[h * C:(h + 1) * C]
            ys.append(y4)
            bwq, add = upd[i][s]
            hs_ref[slot, hh] = g0 * ops[i]['p_end'][s * C:s * C + 1, :] - _mm_nt(g0b, bwq) + add
        y_parts[(sub, hh)] = ys[0] if G == 1 else jnp.concatenate(ys, axis=0)
    y_subs = [jnp.concatenate([y_parts[(sub, 0)], y_parts[(sub, 1)]], axis=1) for sub in range(NSUB)]
    y = y_subs[0] if NSUB == 1 else jnp.concatenate(y_subs, axis=0)

    mean = _seg_sum(y) * (1.0 / 64.0)
    yc = y - mean
    var = _seg_sum(yc * yc) * (1.0 / 64.0)
    yn = yc * lax.rsqrt(var + RWKV_GN_EPS) * lnw_ref[...] + lnb_ref[...]
    bonus = _seg_sum(r * kmod * rk_ref[...]) * v
    y_ref[...] = ((yn + bonus) * gate).astype(y_ref.dtype)

    @pl.when(c == pl.num_programs(1) - 1)
    def _():
        for slot in range(n_slots):
            for hh in range(2):
                for h in range(4):
                    hout_ref[slot, 4 * hh + h] = hs_ref[slot, hh, 64 * h:64 * h + 64, 64 * h:64 * h + 64]


def _row(n):
    return pl.BlockSpec((1, n), lambda *_: (0, 0))


def _full(shape):
    nd = len(shape)
    return pl.BlockSpec(shape, lambda *_: (0,) * nd)


def _any():
    return pl.BlockSpec(memory_space=pl.ANY)


def _winprep(w_in):
    depth, d, cols = w_in.shape
    tk = TILES['prep_tk']
    return pl.pallas_call(
        _winprep_kernel,
        grid=(depth, d // tk),
        in_specs=[pl.BlockSpec((None, tk, cols), lambda l, i: (l, i, 0))],
        out_specs=pl.BlockSpec((None, tk, PW), lambda l, i: (l, i, 0)),
        out_shape=jax.ShapeDtypeStruct((depth, d, PW), BF16),
        compiler_params=_cparams(("parallel", "parallel")),
        name="winprep",
    )(w_in)


def _inproj(x, g, w_all, l):
    t, tm, tn = x.shape[0], TILES['in_tm'], TILES['in_tn']
    return pl.pallas_call(
        _inproj_kernel,
        grid=(t // tm, PW // tn),
        in_specs=[pl.BlockSpec((tm, D_MODEL), lambda i, j: (i, 0)), _row(D_MODEL),
                  pl.BlockSpec((None, D_MODEL, tn), lambda i, j: (l, 0, j))],
        out_specs=pl.BlockSpec((tm, tn), lambda i, j: (i, j)),
        out_shape=jax.ShapeDtypeStruct((t, PW), F32),
        scratch_shapes=[pltpu.VMEM((tm, D_MODEL), BF16)],
        compiler_params=_cparams(("parallel", "arbitrary")),
        name="inproj",
    )(x, g, w_all)


def _outproj(ys, w_all, l, x):
    t, tm = x.shape[0], TILES['out_tm']
    yspec = pl.BlockSpec((tm, GW), lambda i: (i, 0))
    xspec = pl.BlockSpec((tm, D_MODEL), lambda i: (i, 0))
    return pl.pallas_call(
        _outproj_kernel,
        grid=(t // tm,),
        in_specs=[yspec, yspec, yspec, yspec,
                  pl.BlockSpec((None, D_MODEL, D_MODEL), lambda i: (l, 0, 0)), xspec],
        out_specs=xspec,
        out_shape=jax.ShapeDtypeStruct((t, D_MODEL), F32),
        compiler_params=_cparams(("parallel",)),
        name="outproj",
    )(*ys, w_all, x)


def _mlp(x, g, w1_all, w2_all, l, g_final, final):
    t, tm, tf = x.shape[0], TILES['mlp_tm'], TILES['mlp_tf']
    xspec = pl.BlockSpec((tm, D_MODEL), lambda i, k: (i, 0))
    return pl.pallas_call(
        functools.partial(_mlp_kernel, final=final),
        grid=(t // tm, D_FF // tf),
        in_specs=[xspec, _row(D_MODEL),
                  pl.BlockSpec((None, D_MODEL, tf), lambda i, k: (l, 0, k)),
                  pl.BlockSpec((None, tf, D_MODEL), lambda i, k: (l, k, 0)),
                  _row(D_MODEL)],
        out_specs=xspec,
        out_shape=jax.ShapeDtypeStruct((t, D_MODEL), F32),
        scratch_shapes=[pltpu.VMEM((tm, D_MODEL), BF16)],
        compiler_params=_cparams(("parallel", "arbitrary")),
        name="mlp",
    )(x, g, w1_all, w2_all, g_final)


def _mixer_call(kernel_fn, name, p, t_rows, n_seq, n_blk, rows, row0, col_specs, state_in, params, y_prev,
                state_out, scratch, tables=(), layer=0):
    def rowmap(cb):
        return lambda s, c: (row0 + s * n_blk + c, cb)

    in_specs = [pl.BlockSpec((rows, w), rowmap(cb)) for w, cb in col_specs]
    args = [p] * len(col_specs)
    for arr, blk in state_in:
        nd = len(blk)
        in_specs.append(pl.BlockSpec((None,) + blk, lambda s, c, nd=nd: (layer, s) + (0,) * (nd - 1)))
        args.append(arr)
    for tab, blk in tables:
        in_specs.append(pl.BlockSpec(blk, lambda s, c: (c, 0)))
        args.append(tab)
    for arr in params:
        in_specs.append(pl.BlockSpec((None,) + arr.shape[1:], lambda s, c, nd=arr.ndim: (layer,) + (0,) * (nd - 1)))
        args.append(arr)
    aliases = {}
    if y_prev is not None:
        aliases = {len(args): 0}
        in_specs.append(_any())
        args.append(y_prev)
    out_specs = [pl.BlockSpec((rows, GW), lambda s, c: (row0 + s * n_blk + c, 0))]
    out_shape = [jax.ShapeDtypeStruct((t_rows, GW), BF16)]
    for shape, blk in state_out:
        nd = len(blk)
        out_specs.append(pl.BlockSpec(blk, lambda s, c, nd=nd: (s,) + (0,) * (nd - 1)))
        out_shape.append(jax.ShapeDtypeStruct(shape, F32))
    return pl.pallas_call(
        kernel_fn,
        grid=(n_seq, n_blk),
        in_specs=in_specs,
        out_specs=out_specs,
        out_shape=out_shape,
        scratch_shapes=scratch,
        input_output_aliases=aliases,
        compiler_params=_cparams(("arbitrary", "arbitrary")),
        name=name,
    )(*args)


def _run_ssm(p, dims, l, params, buf_s, h0_s):
    t, bp, lp, bs, ls = dims
    tp = bp * lp
    q = TILES['ssm_q']
    cols = [(SSM_XBC, CB_XBC), (GW, CB_Z), (128, CB_DT)]
    hshape = lambda n: [((n, SSM_HEADS, 64, SSM_STATE), (1, SSM_HEADS, 64, SSM_STATE))]
    scratch = lambda rows: [pltpu.VMEM((HIST + rows, SSM_XBC), F32), pltpu.VMEM((2, SSM_STATE, 256), F32)]
    y, h_p = _mixer_call(functools.partial(_ssm_kernel, Q=q, has_state=False), "ssm_prompt", p, t, bp, lp // q, q,
                         0, cols, [], params, None, hshape(bp), scratch(q), layer=l)
    y, h_s = _mixer_call(functools.partial(_ssm_kernel, Q=ls, has_state=True), "ssm_sample", p, t, bs, 1, ls,
                         tp // ls, cols, [(buf_s, (1, HIST, SSM_XBC)), (h0_s, (1, SSM_HEADS, 64, SSM_STATE))],
                         params, y, hshape(bs), scratch(ls), layer=l)
    return y, h_p, h_s


def _run_lru(p, dims, l, params, buf_s, h0_s):
    t, bp, lp, bs, ls = dims
    tp = bp * lp
    r = TILES['lru_r']
    cols = [(2 * GW, CB_LRU)]
    hshape = lambda n: [((n, 1, GW), (1, 1, GW))]
    scratch = lambda rows: [pltpu.VMEM((HIST + rows, GW), F32), pltpu.VMEM((1, GW), F32)]
    y, h_p = _mixer_call(functools.partial(_lru_kernel, R=r, has_state=False), "lru_prompt", p, t, bp, lp // r, r, 0,
                         cols, [], params, None, hshape(bp), scratch(r), layer=l)
    y, h_s = _mixer_call(functools.partial(_lru_kernel, R=ls, has_state=True), "lru_sample", p, t, bs, 1, ls,
                         tp // ls, cols, [(buf_s, (1, HIST, GW)), (h0_s, (1, 1, GW))], params, y, hshape(bs),
                         scratch(ls), layer=l)
    return y, h_p[:, 0], h_s[:, 0]


def _run_swa(p, dims, l, rope_p, rope_s, sinks, k0_s, v0_s):
    t, bp, lp, bs, ls = dims
    tp = bp * lp
    q, ns = TILES['swa_q'], TILES['swa_ns']
    cols = [(GW, CB_Q), (256, CB_KV)]
    cshape = lambda n, g: [((n, WINDOW, 128), (g, WINDOW, 128))] * 2
    scratch = lambda g: [pltpu.VMEM((g, WINDOW, 128), F32), pltpu.VMEM((g, WINDOW, 128), F32)]
    y, k_p, v_p = _mixer_call(functools.partial(_swa_kernel, QB=q, NS=1, has_state=False), "swa_prompt", p, t, bp,
                              lp // q, q, 0, cols, [], [sinks], None, cshape(bp, 1), scratch(1),
                              tables=[(tab, (q, GW)) for tab in rope_p], layer=l)
    y, k_s, v_s = _mixer_call(functools.partial(_swa_kernel, QB=ls, NS=ns, has_state=True), "swa_sample", p, t,
                              bs // ns, 1, ns * ls, tp // (ns * ls), cols,
                              [(k0_s, (ns, WINDOW, 128)), (v0_s, (ns, WINDOW, 128))], [sinks], y, cshape(bs, ns),
                              scratch(ns), tables=[(tab, (ls, GW)) for tab in rope_s], layer=l)
    return y, k_p, v_p, k_s, v_s


def _run_rwkv(p, dims, l, params, sh_rows_s, s0_s):
    t, bp, lp, bs, ls = dims
    tp = bp * lp
    c, nsub = TILES['rwkv_c'], TILES['rwkv_sub']
    g, nsub_s = TILES['rwkv_g'], TILES['rwkv_sub_s']
    cols = [(GW, CB_R), (GW, CB_K), (GW, CB_V), (256, CB_LORA)]
    sshape = lambda n, blk: [((n, 8, 64, 64), (blk, 8, 64, 64))]
    scratch = lambda slots: [pltpu.VMEM((slots, 2, 256, 256), F32), pltpu.VMEM((HIST, RWKV_COLS), F32)]
    rows_p = nsub * c
    y, s_p = _mixer_call(functools.partial(_rwkv_kernel, NSUB=nsub, G=1, C=c, has_state=False), "rwkv_prompt", p, t,
                         bp, lp // rows_p, rows_p, 0, cols, [], params, None, sshape(bp, 1), scratch(1), layer=l)
    nseq = nsub_s * g
    rows_s = nseq * ls
    y, s_s = _mixer_call(functools.partial(_rwkv_kernel, NSUB=nsub_s, G=g, C=ls, has_state=True), "rwkv_sample", p, t,
                         bs // nseq, 1, rows_s, tp // rows_s, cols,
                         [(sh_rows_s, (rows_s, RWKV_COLS)), (s0_s, (nseq, 8, 64, 64))], params, y, sshape(bs, nseq),
                         scratch(nseq), layer=l)
    return y, s_p, s_s


def _rope_tables(pos):
    half = ROT_DIM // 2
    inv = ROPE_THETA ** (-(jnp.arange(half, dtype=F32) * 2.0) / ROT_DIM)
    ang = pos.astype(F32)[:, None] * inv
    cos, sin = jnp.cos(ang), jnp.sin(ang)
    n = pos.shape[0]
    zeros = jnp.zeros((n, half), F32)
    rest = SWA_HEADDIM - ROT_DIM
    cos64 = jnp.concatenate([cos, cos, jnp.ones((n, rest), F32)], axis=1)
    sa64 = jnp.concatenate([-sin, zeros, jnp.zeros((n, rest), F32)], axis=1)
    sb64 = jnp.concatenate([zeros, sin, jnp.zeros((n, rest), F32)], axis=1)
    return tuple(jnp.tile(t, (1, GW // SWA_HEADDIM)) for t in (cos64, sa64, sb64))


def _pad_rows_front(x, axis, total):
    pad = [(0, 0)] * x.ndim
    pad[axis] = (total - x.shape[axis], 0)
    return jnp.pad(x, pad)


def kernel(x_prompt, x_sample, state_ssm, state_ssm_conv, state_rwkv, state_rwkv_shift, cache_swa_k,
           cache_swa_v, state_lru, state_lru_conv, norm_mix, w_in, ssm_conv_w, ssm_conv_b, ssm_dt_bias,
           ssm_a_log, ssm_d, ssm_norm, rwkv_mu, rwkv_w0, rwkv_w2, rwkv_a0, rwkv_a2, rwkv_g2, rwkv_k_k,
           rwkv_k_a, rwkv_r_k, rwkv_ln_w, rwkv_ln_b, swa_sinks, lru_conv_w, lru_conv_b, lru_wa, lru_ba,
           lru_wi, lru_bi, lru_lambda, w_out, norm_mlp, mlp_w1, mlp_w2, norm_final):
    bp, lp, _ = x_prompt.shape
    bs, ls, _ = x_sample.shape
    tp, ts = bp * lp, bs * ls
    t = tp + ts
    dims = (t, bp, lp, bs, ls)
    depth = w_in.shape[0]

    w_in_r = _winprep(w_in.astype(BF16))
    w_out_b = w_out.astype(BF16)
    w1_b = mlp_w1.astype(BF16)
    w2_b = mlp_w2.astype(BF16)
    eye8 = jnp.eye(8, dtype=F32)
    lru_wa_d = jnp.einsum('lncd,nm->lncmd', lru_wa, eye8).reshape(depth, GW, GW).astype(BF16)
    lru_wi_d = jnp.einsum('lncd,nm->lncmd', lru_wi, eye8).reshape(depth, GW, GW).astype(BF16)
    z64 = jnp.zeros((depth, 64, GW), F32)
    rwkv_w2_p = jnp.concatenate([rwkv_w2, z64], axis=1).astype(BF16)
    rwkv_a2_p = jnp.concatenate([z64, rwkv_a2], axis=1).astype(BF16)
    rwkv_g2_b = rwkv_g2.astype(BF16)
    dtb_p = jnp.pad(ssm_dt_bias, ((0, 0), (0, 128 - SSM_HEADS)))
    alog_p = jnp.pad(ssm_a_log, ((0, 0), (0, 128 - SSM_HEADS)))
    d_e = jnp.repeat(ssm_d, 64, axis=-1)
    sinks_b = jnp.broadcast_to(swa_sinks[:, :, None], (depth, 8, 128))
    rope_p = _rope_tables(jnp.arange(lp))
    rope_s = _rope_tables(PAST_LEN + jnp.arange(ls))

    def r2(a, l):
        return a[l][None, :]

    def rows(a):
        return a[:, None, :]

    ssm_params = [ssm_conv_w, rows(ssm_conv_b), rows(dtb_p), rows(alog_p), rows(d_e), rows(ssm_norm)]
    rwkv_params = [rows(rwkv_mu), rows(rwkv_w0), rwkv_w2_p, rows(rwkv_a0), rwkv_a2_p, rwkv_g2_b, rows(rwkv_k_k),
                   rows(rwkv_k_a), rwkv_r_k.reshape(depth, 1, GW), rows(rwkv_ln_w), rows(rwkv_ln_b)]
    lru_params = [lru_conv_w, rows(lru_conv_b), lru_wa_d, rows(lru_ba), lru_wi_d, rows(lru_bi), rows(lru_lambda)]

    ssm_buf = _pad_rows_front(state_ssm_conv, 2, HIST)
    lru_buf = _pad_rows_front(state_lru_conv, 2, HIST)
    lru_h0 = state_lru[:, :, None, :]
    sh_rows = jnp.pad(state_rwkv_shift[:, :, None, :], ((0, 0), (0, 0), (0, ls - 1), (0, 0))).reshape(depth, ts, RWKV_COLS)
    swa_k0 = cache_swa_k.reshape(depth, bs, WINDOW, 128)
    swa_v0 = cache_swa_v.reshape(depth, bs, WINDOW, 128)

    x = jnp.concatenate([x_prompt.reshape(tp, D_MODEL), x_sample.reshape(ts, D_MODEL)], axis=0)

    names = ('ssm_p', 'ssm_s', 'conv_p', 'conv_s', 'rwkv_p', 'rwkv_s', 'shift_p', 'shift_s',
             'k_p', 'k_s', 'v_p', 'v_s', 'lru_p', 'lru_s', 'lconv_p', 'lconv_s')
    outs = {n: [] for n in names}
    for l in range(depth):
        p = _inproj(x, r2(norm_mix, l), w_in_r, l)

        y_ssm, hs_p, hs_s = _run_ssm(p, dims, l, ssm_params, ssm_buf, state_ssm)
        y_rwkv, s_p, s_s = _run_rwkv(p, dims, l, rwkv_params, sh_rows, state_rwkv)
        y_swa, k_p, v_p, k_s, v_s = _run_swa(p, dims, l, rope_p, rope_s, sinks_b, swa_k0, swa_v0)
        y_lru, lh_p, lh_s = _run_lru(p, dims, l, lru_params, lru_buf, lru_h0)

        x1 = _outproj([y_ssm, y_rwkv, y_swa, y_lru], w_out_b, l, x)
        x = _mlp(x1, r2(norm_mlp, l), w1_b, w2_b, l, norm_final[None, :], l + 1 == depth)

        def tail_p(n, c0, c1):
            return jnp.stack([p[b * lp + lp - n:b * lp + lp, c0:c1] for b in range(bp)], axis=0)

        ps = p[tp:].reshape(bs, ls, PW)
        lora0 = CB_LORA * 256
        outs['ssm_p'].append(hs_p)
        outs['ssm_s'].append(hs_s)
        outs['conv_p'].append(tail_p(3, 0, SSM_XBC))
        outs['conv_s'].append(ps[:, ls - 3:, 0:SSM_XBC])
        outs['rwkv_p'].append(s_p)
        outs['rwkv_s'].append(s_s)
        outs['shift_p'].append(jnp.concatenate([tail_p(1, 5 * GW, 8 * GW), tail_p(1, lora0, lora0 + 256)], axis=-1)[:, 0])
        outs['shift_s'].append(jnp.concatenate([ps[:, ls - 1, 5 * GW:8 * GW], ps[:, ls - 1, lora0:lora0 + 256]], axis=-1))
        outs['k_p'].append(k_p.reshape(bp, WINDOW, 2, SWA_HEADDIM))
        outs['k_s'].append(k_s.reshape(bs, WINDOW, 2, SWA_HEADDIM))
        outs['v_p'].append(v_p.reshape(bp, WINDOW, 2, SWA_HEADDIM))
        outs['v_s'].append(v_s.reshape(bs, WINDOW, 2, SWA_HEADDIM))
        outs['lru_p'].append(lh_p)
        outs['lru_s'].append(lh_s)
        outs['lconv_p'].append(tail_p(3, SSM_XBC, SSM_XBC + GW))
        outs['lconv_s'].append(ps[:, ls - 3:, SSM_XBC:SSM_XBC + GW])

    st = {n: jnp.stack(v, axis=0) for n, v in outs.items()}
    return (x[:tp].reshape(bp, lp, D_MODEL), x[tp:].reshape(bs, ls, D_MODEL),
            st['ssm_p'], st['ssm_s'], st['conv_p'], st['conv_s'], st['rwkv_p'], st['rwkv_s'],
            st['shift_p'], st['shift_s'], st['k_p'], st['k_s'], st['v_p'], st['v_s'],
            st['lru_p'], st['lru_s'], st['lconv_p'], st['lconv_s'])
```

```python
import functools
import math

import jax
import jax.numpy as jnp
from jax import lax
from jax.experimental import pallas as pl
from jax.experimental.pallas import tpu as pltpu

F32 = jnp.float32
BF16 = jnp.bfloat16

D_MODEL = 2048
GW = 512
RMS_EPS = 1e-6
CONV_W = 4
SSM_HEADS = 8
SSM_STATE = 128
SSM_XBC = 1024
RWKV_COLS = 1792
RWKV_GN_EPS = 64e-5
SWA_HEADDIM = 64
WINDOW = 128
ROT_DIM = 16
ROPE_THETA = 500000.0
LRU_C = 8.0
D_FF = 4 * D_MODEL
PAST_LEN = 16384

PW = 5376
CB_XBC, CB_LRU = 0, 1
CB_Z, CB_R, CB_K, CB_V, CB_Q = 4, 5, 6, 7, 8
CB_KV, CB_LORA = 18, 19
CB_DT = 40

HIST = 8
VMEM_LIMIT = 56 * 1024 * 1024

TILES = dict(
    prep_tk=256, in_tm=1408, in_tn=768, out_tm=704, mlp_tm=704, mlp_tf=1024,
    ssm_q=128, ssm_nb=4, swa_q=128, swa_nb=4, lru_r=256, rwkv_c=64, rwkv_sub=8,
    ssm_ns=4, swa_ns=4, rwkv_g=4, rwkv_sub_s=2,
)


def _cparams(sem):
    return pltpu.CompilerParams(dimension_semantics=sem, vmem_limit_bytes=VMEM_LIMIT)


def _mm(a, b):
    return jnp.dot(a.astype(BF16), b.astype(BF16), preferred_element_type=F32)


def _mm_nt(a, b):
    return lax.dot_general(a.astype(BF16), b.astype(BF16), (((1,), (1,)), ((), ())),
                           preferred_element_type=F32)


def _mm_tn(a, b):
    return lax.dot_general(a.astype(BF16), b.astype(BF16), (((0,), (0,)), ((), ())),
                           preferred_element_type=F32)


def _split3(x):
    x1 = x.astype(BF16)
    r = x - x1.astype(F32)
    x2 = r.astype(BF16)
    r = r - x2.astype(F32)
    return x1, x2, r.astype(BF16)


def _mm_exact_rhs(a, sel):
    p1, p2, p3 = _split3(a)
    return _mm(p1, sel) + _mm(p2, sel) + _mm(p3, sel)


def _mm_split2_rhs(a, sel):
    p1 = a.astype(BF16)
    p2 = (a - p1.astype(F32)).astype(BF16)
    return _mm(p1, sel) + _mm(p2, sel)


def _mm_exact_lhs(sel, b):
    p1, p2, p3 = _split3(b)
    return _mm(sel, p1) + _mm(sel, p2) + _mm(sel, p3)


def _mm_nt_exact_lhs(sel, b):
    p1, p2, p3 = _split3(b)
    return _mm_nt(sel, p1) + _mm_nt(sel, p2) + _mm_nt(sel, p3)


def _softplus(x):
    return jnp.maximum(x, 0.0) + jnp.log1p(jnp.exp(-jnp.abs(x)))


def _sigmoid(x):
    return 1.0 / (1.0 + jnp.exp(-x))


def _iota(shape, dim):
    return lax.broadcasted_iota(jnp.int32, shape, dim)


def _ind(mask):
    return jnp.where(mask, 1.0, 0.0)


def _rms(x, g):
    ms = jnp.mean(x * x, axis=-1, keepdims=True)
    return x * lax.rsqrt(ms + RMS_EPS) * g


def _conv_from_hist(ext_ref, rows, cw_ref, cb_ref):
    acc = cb_ref[...] + cw_ref[0:1, :] * ext_ref[HIST - 3:HIST - 3 + rows, :]
    for j in range(1, CONV_W):
        acc = acc + cw_ref[j:j + 1, :] * ext_ref[HIST - 3 + j:HIST - 3 + j + rows, :]
    return acc


def _winprep_kernel(w_ref, o_ref):
    o_rwkv = GW + SSM_XBC + SSM_HEADS
    o_swa = o_rwkv + RWKV_COLS
    o_lru = o_swa + GW + 256
    sections = ((GW, SSM_XBC), (o_lru, 2 * GW), (0, GW), (o_rwkv, 3 * GW), (o_swa, GW), (o_swa + GW, 256),
                (o_rwkv + 3 * GW, 256), (GW + SSM_XBC, SSM_HEADS))
    dst = 0
    for src, width in sections:
        o_ref[:, dst:dst + width] = w_ref[:, src:src + width]
        dst += width
    o_ref[:, dst:PW] = jnp.zeros((o_ref.shape[0], PW - dst), BF16)


def _inproj_kernel(x_ref, g_ref, w_ref, o_ref, h_ref):
    @pl.when(pl.program_id(1) == 0)
    def _():
        h_ref[...] = _rms(x_ref[...], g_ref[...]).astype(BF16)

    o_ref[...] = jnp.dot(h_ref[...], w_ref[...], preferred_element_type=F32)


def _outproj_kernel(y0_ref, y1_ref, y2_ref, y3_ref, w_ref, x_ref, x1_ref):
    acc = x_ref[...]
    for i, y_ref in enumerate((y0_ref, y1_ref, y2_ref, y3_ref)):
        acc = acc + jnp.dot(y_ref[...], w_ref[i * GW:(i + 1) * GW, :], preferred_element_type=F32)
    x1_ref[...] = acc


def _mlp_kernel(x_ref, g_ref, w1_ref, w2_ref, gf_ref, xo_ref, h_ref, *, final):
    k = pl.program_id(1)

    @pl.when(k == 0)
    def _():
        x = x_ref[...]
        h_ref[...] = _rms(x, g_ref[...]).astype(BF16)
        xo_ref[...] = x

    a = jnp.dot(h_ref[...], w1_ref[...], preferred_element_type=F32)
    a = jnp.square(jnp.maximum(a, 0.0)).astype(BF16)
    xo_ref[...] += jnp.dot(a, w2_ref[...], preferred_element_type=F32)

    if final:
        @pl.when(k == pl.num_programs(1) - 1)
        def _():
            xo_ref[...] = _rms(xo_ref[...], gf_ref[...])


def _ssm_kernel(*refs, Q, NS, has_state):
    if has_state:
        (xbc_ref, z_ref, dt_ref, buf_ref, h0_ref, cw_ref, cb_ref, dtb_ref, alog_ref, de_ref, nw_ref,
         _, y_ref, hout_ref, ext_ref, h_ref) = refs
    else:
        (xbc_ref, z_ref, dt_ref, cw_ref, cb_ref, dtb_ref, alog_ref, de_ref, nw_ref,
         y_ref, hout_ref, ext_ref, h_ref) = refs
    c = pl.program_id(1)
    n_slots = NS if has_state else 1

    @pl.when(c == 0)
    def _():
        if has_state:
            for slot in range(n_slots):
                ext_ref[slot, 0:HIST, :] = buf_ref[slot]
                for g in range(2):
                    h_ref[slot, g] = h0_ref[slot, 4 * g:4 * g + 4].reshape(256, SSM_STATE).T
        else:
            ext_ref[0, 0:HIST, :] = jnp.zeros((HIST, SSM_XBC), F32)
            h_ref[...] = jnp.zeros(h_ref.shape, F32)

    a_neg = jnp.where(_iota((1, 128), 1) < SSM_HEADS, -jnp.exp(alog_ref[...]), 0.0)
    tri = (_iota((Q, Q), 0) >= _iota((Q, Q), 1))
    tri16 = _ind(tri).astype(BF16)
    expand = _ind(_iota((128, GW), 0) == (_iota((128, GW), 1) >> 6)).astype(BF16)
    sel = _ind(_iota((SSM_HEADS, 128), 0) == _iota((SSM_HEADS, 128), 1)).astype(BF16)
    lanehead = _iota((1, 256), 1) >> 6

    for s in range(NS):
        slot = s if has_state else 0
        rows = slice(s * Q, (s + 1) * Q)
        ext = ext_ref.at[slot]
        ext[HIST:HIST + Q, :] = xbc_ref[rows, :]
        pre = _conv_from_hist(ext, Q, cw_ref, cb_ref)
        ext[0:HIST, :] = ext[Q:Q + HIST, :]
        xbc = pre * _sigmoid(pre)
        x = xbc[:, 0:GW]
        bm = xbc[:, GW:GW + 256]
        cm = xbc[:, GW + 256:GW + 512]

        dt = _softplus(dt_ref[rows, :] + dtb_ref[...])
        cum = _mm_exact_lhs(tri16, dt * a_neg)
        dt_e = _mm_exact_rhs(dt, expand)
        cum_e = _mm_exact_rhs(cum, expand)
        cum_t = _mm_nt_exact_lhs(sel, cum)
        last = cum_e[Q - 1:Q, :]
        expc_e = jnp.exp(cum_e)
        tail_e = jnp.exp(last - cum_e)
        dtot_e = jnp.exp(last)
        dx = dt_e * x

        ys = []
        for g in range(2):
            bg = bm[:, g * 128:(g + 1) * 128]
            cg = cm[:, g * 128:(g + 1) * 128]
            cb = _mm_nt(cg, bg)
            h_t = h_ref[slot, g]
            yg = _mm(cg, h_t) * expc_e[:, g * 256:(g + 1) * 256]
            dxg = dx[:, g * 256:(g + 1) * 256]
            for hh in range(4):
                h = 4 * g + hh
                decay = jnp.exp(jnp.minimum(cum[:, h:h + 1] - cum_t[h:h + 1, :], 0.0))
                m = jnp.where(tri, cb * decay, 0.0)
                yg = yg + _mm(m, jnp.where(lanehead == hh, dxg, 0.0))
            ys.append(yg)
            h_ref[slot, g] = (h_t * dtot_e[:, g * 256:(g + 1) * 256]
                              + _mm_tn(bg, dxg * tail_e[:, g * 256:(g + 1) * 256]))

        y = jnp.concatenate(ys, axis=1) + de_ref[...] * x
        z = z_ref[rows, :]
        y = y * (z * _sigmoid(z))
        outs = []
        for g in range(2):
            yg = y[:, g * 256:(g + 1) * 256]
            outs.append(yg * lax.rsqrt(jnp.mean(yg * yg, axis=-1, keepdims=True) + RMS_EPS))
        y_ref[rows, :] = (jnp.concatenate(outs, axis=1) * nw_ref[...]).astype(y_ref.dtype)

    @pl.when(c == pl.num_programs(1) - 1)
    def _():
        for slot in range(n_slots):
            for g in range(2):
                hout_ref[slot, 4 * g:4 * g + 4] = h_ref[slot, g].T.reshape(4, 64, SSM_STATE)


def _lru_kernel(*refs, R, has_state):
    if has_state:
        (u_ref, buf_ref, h0_ref, cw_ref, cb_ref, wa_ref, ba_ref, wi_ref, bi_ref, lam_ref,
         _, y_ref, hout_ref, ext_ref, h_ref) = refs
    else:
        (u_ref, cw_ref, cb_ref, wa_ref, ba_ref, wi_ref, bi_ref, lam_ref,
         y_ref, hout_ref, ext_ref, h_ref) = refs
    c = pl.program_id(1)

    @pl.when(c == 0)
    def _():
        if has_state:
            ext_ref[0:HIST, :] = buf_ref[0]
            h_ref[...] = h0_ref[0]
        else:
            ext_ref[0:HIST, :] = jnp.zeros((HIST, GW), F32)
            h_ref[...] = jnp.zeros(h_ref.shape, F32)

    u = u_ref[...]
    gate = u[:, GW:]
    ext_ref[HIST:HIST + R, :] = u[:, :GW]
    xc = _conv_from_hist(ext_ref, R, cw_ref, cb_ref)
    ext_ref[0:HIST, :] = ext_ref[R:R + HIST, :]

    rg = _sigmoid(_mm(xc, wa_ref[...]) + ba_ref[...])
    ig = _sigmoid(_mm(xc, wi_ref[...]) + bi_ref[...])
    log_a = -LRU_C * rg * _softplus(-lam_ref[...])
    a = jnp.exp(log_a)
    th = jnp.tanh(log_a)
    b = jnp.sqrt(-2.0 * th / (1.0 - th)) * (ig * xc)

    row = _iota((R, 1), 0)
    s = 1
    while s < R:
        keep = row >= s
        a_prev = jnp.where(keep, pltpu.roll(a, s, 0), 1.0)
        b_prev = jnp.where(keep, pltpu.roll(b, s, 0), 0.0)
        b = b + a * b_prev
        a = a * a_prev
        s *= 2
    h = b + a * h_ref[...]
    h_ref[...] = h[R - 1:R, :]
    hout_ref[0] = h[R - 1:R, :]

    gelu = 0.5 * gate * (1.0 + jnp.tanh(math.sqrt(2.0 / math.pi) * (gate + 0.044715 * (gate * gate * gate))))
    y_ref[...] = (h * gelu).astype(y_ref.dtype)


def _swa_kernel(*refs, QB, NS, has_state):
    if has_state:
        (q_ref, kv_ref, kc_ref, vc_ref, cos_ref, sa_ref, sb_ref, sink_ref,
         _, y_ref, kout_ref, vout_ref, hk_ref, hv_ref) = refs
    else:
        (q_ref, kv_ref, cos_ref, sa_ref, sb_ref, sink_ref,
         y_ref, kout_ref, vout_ref, hk_ref, hv_ref) = refs
    c = pl.program_id(1)
    nk = WINDOW + QB
    lq = QB.bit_length() - 1

    @pl.when(c == 0)
    def _():
        if has_state:
            hk_ref[...] = kc_ref[...]
            hv_ref[...] = vc_ref[...]
        else:
            hk_ref[...] = jnp.zeros(hk_ref.shape, F32)
            hv_ref[...] = jnp.zeros(hv_ref.shape, F32)

    def rope(x, width, trows):
        cos, sa, sb = cos_ref[trows, 0:width], sa_ref[trows, 0:width], sb_ref[trows, 0:width]
        return x * cos + pltpu.roll(x, width - ROT_DIM // 2, 1) * sa + pltpu.roll(x, ROT_DIM // 2, 1) * sb

    kj = _iota((nk, 4 * QB), 0)
    qi = _iota((nk, 4 * QB), 1) & (QB - 1)
    neg_mask = jnp.where(kj >= qi, jnp.where(kj <= qi + WINDOW, 0.0, -jnp.inf), -jnp.inf)
    neg_mask0 = neg_mask if has_state else jnp.where(c == 0, jnp.where(kj >= WINDOW, neg_mask, -jnp.inf), neg_mask)
    upper = _iota((1, 128), 1) >= SWA_HEADDIM
    lower = _iota((1, 128), 1) < SWA_HEADDIM
    chead = _iota((1, 4 * QB), 1) >> lq
    sink_rows = []
    for g in range(2):
        sr = sink_ref[4 * g + 3:4 * g + 4, 0:1]
        for hh in (2, 1, 0):
            sr = jnp.where(chead == hh, sink_ref[4 * g + hh:4 * g + hh + 1, 0:1], sr)
        sink_rows.append(sr)

    for s in range(NS):
        slot = s if has_state else 0
        rows = slice(s * QB, (s + 1) * QB)
        trows = slice(0, QB) if has_state else rows
        mask = neg_mask0 if s == 0 else neg_mask
        q = rope(q_ref[rows, :], GW, trows) * (SWA_HEADDIM ** -0.5)
        kv = kv_ref[rows, :]
        k_new = rope(kv[:, 0:128], 128, trows)
        v_new = kv[:, 128:256]
        keys = jnp.concatenate([hk_ref[slot], k_new], axis=0)
        vals = jnp.concatenate([hv_ref[slot], v_new], axis=0)

        tiles = [None] * 4
        for g in range(2):
            own = upper if g == 1 else lower
            kd = jnp.where(own, keys, pltpu.roll(keys, SWA_HEADDIM, 1))
            vd = jnp.where(own, vals, pltpu.roll(vals, SWA_HEADDIM, 1))
            parts = []
            for hh in range(4):
                h = 4 * g + hh
                mine = upper if h % 2 == 1 else lower
                parts.append(jnp.where(mine, q[:, 128 * (h // 2):128 * (h // 2 + 1)], 0.0))
            sc = _mm_nt(kd, jnp.concatenate(parts, axis=0)) + mask
            m = jnp.maximum(jnp.max(sc, axis=0, keepdims=True), sink_rows[g])
            e = jnp.exp(sc - m)
            den = jnp.sum(e, axis=0, keepdims=True) + jnp.exp(sink_rows[g] - m)
            o = _mm_tn(e / den, vd)
            for hh in range(4):
                h = 4 * g + hh
                oh = o[hh * QB:(hh + 1) * QB]
                tiles[h // 2] = oh if h % 2 == 0 else jnp.where(upper, oh, tiles[h // 2])
        y_ref[rows, :] = jnp.concatenate(tiles, axis=1).astype(y_ref.dtype)

        if QB == WINDOW:
            hk_ref[slot] = k_new
            hv_ref[slot] = v_new
        else:
            hk_ref[slot] = keys[QB:, :]
            hv_ref[slot] = vals[QB:, :]
    kout_ref[...] = hk_ref[...]
    vout_ref[...] = hv_ref[...]


def _seg_sum(x):
    ones = _ind((_iota((256, 256), 0) >> 6) == (_iota((256, 256), 1) >> 6)).astype(BF16)
    return jnp.concatenate([_mm_split2_rhs(x[:, 0:256], ones), _mm_split2_rhs(x[:, 256:512], ones)], axis=1)


def _rwkv_kernel(*refs, NSUB, G, C, has_state):
    if has_state:
        (r_ref, k_ref, v_ref, lo_ref, sh_ref, h0_ref, mu_ref, w0_ref, w2_ref, a0_ref, a2_ref, g2_ref,
         kk_ref, ka_ref, rk_ref, lnw_ref, lnb_ref, _, y_ref, hout_ref, hs_ref, carry_ref) = refs
    else:
        (r_ref, k_ref, v_ref, lo_ref, mu_ref, w0_ref, w2_ref, a0_ref, a2_ref, g2_ref,
         kk_ref, ka_ref, rk_ref, lnw_ref, lnb_ref, y_ref, hout_ref, hs_ref, carry_ref) = refs
    c = pl.program_id(1)
    R = G * C
    RB = NSUB * R
    lc = C.bit_length() - 1
    n_slots = NSUB * G if has_state else 1

    @pl.when(c == 0)
    def _():
        if has_state:
            for slot in range(n_slots):
                for hh in range(2):
                    rows = [jnp.pad(h0_ref[slot, 4 * hh + h], ((0, 0), (64 * h, 192 - 64 * h))) for h in range(4)]
                    hs_ref[slot, hh] = jnp.concatenate(rows, axis=0)
        else:
            hs_ref[...] = jnp.zeros(hs_ref.shape, F32)
            carry_ref[...] = jnp.zeros(carry_ref.shape, F32)

    row = _iota((RB, 1), 0)
    first = ((row & (C - 1)) == 0) if has_state else (row == 0)
    mixed = []
    for ref, off, width in ((r_ref, 0, GW), (k_ref, GW, GW), (v_ref, 2 * GW, GW), (lo_ref, 3 * GW, 256)):
        pf = ref[...]
        if has_state:
            fv = sh_ref[:, off:off + width]
        else:
            fv = carry_ref[0:1, off:off + width]
        shifted = jnp.where(first, fv, pltpu.roll(pf, 1, 0))
        mixed.append(pf + (shifted - pf) * mu_ref[:, off:off + width])
        if not has_state:
            carry_ref[0:1, off:off + width] = pf[RB - 1:RB, :]
    r, k, v, lo = mixed

    lo_wa = lo[:, 0:128]
    w_log = -_softplus(-(w0_ref[...] + _mm(jnp.tanh(lo_wa), w2_ref[...]))) - 0.5
    lw = -jnp.exp(w_log)
    a = _sigmoid(a0_ref[...] + _mm(lo_wa, a2_ref[...]))
    gate = _mm(_sigmoid(lo[:, 128:256]), g2_ref[...])
    kk = k * kk_ref[...]
    kmod = k * (1.0 + (a - 1.0) * ka_ref[...])
    kk = kk * lax.rsqrt(jnp.maximum(_seg_sum(kk * kk), 1e-24))
    kb = kk * a

    NB = 4 * R
    lanehead = _iota((1, 256), 1) >> 6
    bi, bj = _iota((NB, NB), 0), _iota((NB, NB), 1)
    same = _ind((bi >> lc) == (bj >> lc))
    strict = same * _ind(bj < bi)
    incl = same * _ind(bj <= bi)
    eye_nb = _ind(bi == bj)
    si, sj = _iota((R, R), 0), _iota((R, R), 1)
    seg = _ind((si >> lc) == (sj >> lc))
    seg_ones = seg.astype(BF16)
    seg_tri = (seg * _ind(sj <= si)).astype(BF16)

    def bd(x):
        parts = []
        for s in range(G):
            xs = x[s * C:(s + 1) * C]
            for h in range(4):
                parts.append(jnp.where(lanehead == h, xs, 0.0))
        return jnp.concatenate(parts, axis=0)

    chains = [(sub, hh) for sub in range(NSUB) for hh in range(2)]
    ops = []
    for sub, hh in chains:
        sr, sl = slice(sub * R, (sub + 1) * R), slice(hh * 256, (hh + 1) * 256)
        lw4 = lw[sr, sl]
        clw = _mm_exact_lhs(seg_tri, lw4)
        clt = clw[C - 1:C, :] if G == 1 else _mm_exact_lhs(seg_ones, lw4)
        e_out = jnp.exp(-clw)
        e_end = jnp.exp(clt - clw)
        b_rq = bd(r[sr, sl] * jnp.exp(clw))
        ops.append(dict(
            kq=bd(kk[sr, sl] * jnp.exp(clw - lw4)).astype(BF16), rq=b_rq, rq16=b_rq.astype(BF16),
            kk=bd(kmod[sr, sl] * e_out).astype(BF16), bk=bd(kb[sr, sl] * e_out).astype(BF16),
            kk2=bd(kmod[sr, sl] * e_end).astype(BF16), bk2=bd(kb[sr, sl] * e_end).astype(BF16),
            v=bd(v[sr, sl]).astype(BF16), p_end=jnp.exp(clt)))

    n_mat = [strict * _mm_nt(o['kq'], o['bk']) for o in ops]
    a_kk = [strict * _mm_nt(o['kq'], o['kk']) for o in ops]
    a_rb = [incl * _mm_nt(o['rq16'], o['bk']) for o in ops]
    a_rk = [incl * _mm_nt(o['rq16'], o['kk']) for o in ops]

    t_inv = [eye_nb - n for n in n_mat]
    pw = n_mat
    for _ in range(lc - 1):
        pw = [_mm(x, x) for x in pw]
        t_inv = [t + _mm(t, x) for t, x in zip(t_inv, pw)]

    av = [_mm(a, o['v']) for a, o in zip(a_kk, ops)]
    wu = [_mm(t, jnp.concatenate([o['kq'], x.astype(BF16)], axis=1)) for t, o, x in zip(t_inv, ops, av)]
    arb_wu = [_mm(a, x) for a, x in zip(a_rb, wu)]
    rq2 = [o['rq'] - x[:, 0:256] for o, x in zip(ops, arb_wu)]
    y_loc = [_mm(a, o['v']) - x[:, 256:512] for a, o, x in zip(a_rk, ops, arb_wu)]
    wu16 = [x.astype(BF16) for x in wu]

    upd = []
    for i, o in enumerate(ops):
        per_seg = []
        for s in range(G):
            rs = slice(s * 4 * C, (s + 1) * 4 * C)
            bwq = _mm_tn(o['bk2'][rs], wu16[i][rs, 0:256])
            add = _mm_tn(o['v'][rs], o['kk2'][rs]) - _mm_tn(wu16[i][rs, 256:512], o['bk2'][rs])
            per_seg.append((bwq.astype(BF16), add))
        upd.append(per_seg)

    y_parts = {}
    for i, (sub, hh) in enumerate(chains):
        ys = []
        for s in range(G):
            rs = slice(s * 4 * C, (s + 1) * 4 * C)
            slot = sub * G + s if has_state else 0
            g0 = hs_ref[slot, hh]
            g0b = g0.astype(BF16)
            yb = _mm_nt(rq2[i][rs], g0b) + y_loc[i][rs]
            y4 = yb[0:C]
            for h in range(1, 4):
                y4 = y4 + yb[h * C:(h + 1) * C]
            ys.append(y4)
            bwq, add = upd[i][s]
            hs_ref[slot, hh] = g0 * ops[i]['p_end'][s * C:s * C + 1, :] - _mm_nt(g0b, bwq) + add
        y_parts[(sub, hh)] = ys[0] if G == 1 else jnp.concatenate(ys, axis=0)
    y_subs = [jnp.concatenate([y_parts[(sub, 0)], y_parts[(sub, 1)]], axis=1) for sub in range(NSUB)]
    y = y_subs[0] if NSUB == 1 else jnp.concatenate(y_subs, axis=0)

    mean = _seg_sum(y) * (1.0 / 64.0)
    yc = y - mean
    var = _seg_sum(yc * yc) * (1.0 / 64.0)
    yn = yc * lax.rsqrt(var + RWKV_GN_EPS) * lnw_ref[...] + lnb_ref[...]
    bonus = _seg_sum(r * kmod * rk_ref[...]) * v
    y_ref[...] = ((yn + bonus) * gate).astype(y_ref.dtype)

    @pl.when(c == pl.num_programs(1) - 1)
    def _():
        for slot in range(n_slots):
            for hh in range(2):
                for h in range(4):
                    hout_ref[slot, 4 * hh + h] = hs_ref[slot, hh, 64 * h:64 * h + 64, 64 * h:64 * h + 64]


def _row(n):
    return pl.BlockSpec((1, n), lambda *_: (0, 0))


def _full(shape):
    nd = len(shape)
    return pl.BlockSpec(shape, lambda *_: (0,) * nd)


def _any():
    return pl.BlockSpec(memory_space=pl.ANY)


def _winprep(w_in):
    depth, d, cols = w_in.shape
    tk = TILES['prep_tk']
    return pl.pallas_call(
        _winprep_kernel,
        grid=(depth, d // tk),
        in_specs=[pl.BlockSpec((None, tk, cols), lambda l, i: (l, i, 0))],
        out_specs=pl.BlockSpec((None, tk, PW), lambda l, i: (l, i, 0)),
        out_shape=jax.ShapeDtypeStruct((depth, d, PW), BF16),
        compiler_params=_cparams(("parallel", "parallel")),
        name="winprep",
    )(w_in)


def _inproj(x, g, w_all, l):
    t, tm, tn = x.shape[0], TILES['in_tm'], TILES['in_tn']
    return pl.pallas_call(
        _inproj_kernel,
        grid=(t // tm, PW // tn),
        in_specs=[pl.BlockSpec((tm, D_MODEL), lambda i, j: (i, 0)), _row(D_MODEL),
                  pl.BlockSpec((None, D_MODEL, tn), lambda i, j: (l, 0, j))],
        out_specs=pl.BlockSpec((tm, tn), lambda i, j: (i, j)),
        out_shape=jax.ShapeDtypeStruct((t, PW), F32),
        scratch_shapes=[pltpu.VMEM((tm, D_MODEL), BF16)],
        compiler_params=_cparams(("parallel", "arbitrary")),
        name="inproj",
    )(x, g, w_all)


def _outproj(ys, w_all, l, x):
    t, tm = x.shape[0], TILES['out_tm']
    yspec = pl.BlockSpec((tm, GW), lambda i: (i, 0))
    xspec = pl.BlockSpec((tm, D_MODEL), lambda i: (i, 0))
    return pl.pallas_call(
        _outproj_kernel,
        grid=(t // tm,),
        in_specs=[yspec, yspec, yspec, yspec,
                  pl.BlockSpec((None, D_MODEL, D_MODEL), lambda i: (l, 0, 0)), xspec],
        out_specs=xspec,
        out_shape=jax.ShapeDtypeStruct((t, D_MODEL), F32),
        compiler_params=_cparams(("parallel",)),
        name="outproj",
    )(*ys, w_all, x)


def _mlp(x, g, w1_all, w2_all, l, g_final, final):
    t, tm, tf = x.shape[0], TILES['mlp_tm'], TILES['mlp_tf']
    xspec = pl.BlockSpec((tm, D_MODEL), lambda i, k: (i, 0))
    return pl.pallas_call(
        functools.partial(_mlp_kernel, final=final),
        grid=(t // tm, D_FF // tf),
        in_specs=[xspec, _row(D_MODEL),
                  pl.BlockSpec((None, D_MODEL, tf), lambda i, k: (l, 0, k)),
                  pl.BlockSpec((None, tf, D_MODEL), lambda i, k: (l, k, 0)),
                  _row(D_MODEL)],
        out_specs=xspec,
        out_shape=jax.ShapeDtypeStruct((t, D_MODEL), F32),
        scratch_shapes=[pltpu.VMEM((tm, D_MODEL), BF16)],
        compiler_params=_cparams(("parallel", "arbitrary")),
        name="mlp",
    )(x, g, w1_all, w2_all, g_final)


def _mixer_call(kernel_fn, name, p, t_rows, n_seq, n_blk, rows, row0, col_specs, state_in, params, y_prev,
                state_out, scratch, tables=(), layer=0):
    def rowmap(cb):
        return lambda s, c: (row0 + s * n_blk + c, cb)

    in_specs = [pl.BlockSpec((rows, w), rowmap(cb)) for w, cb in col_specs]
    args = [p] * len(col_specs)
    for arr, blk in state_in:
        nd = len(blk)
        in_specs.append(pl.BlockSpec((None,) + blk, lambda s, c, nd=nd: (layer, s) + (0,) * (nd - 1)))
        args.append(arr)
    for tab, blk in tables:
        in_specs.append(pl.BlockSpec(blk, lambda s, c: (c, 0)))
        args.append(tab)
    for arr in params:
        in_specs.append(pl.BlockSpec((None,) + arr.shape[1:], lambda s, c, nd=arr.ndim: (layer,) + (0,) * (nd - 1)))
        args.append(arr)
    aliases = {}
    if y_prev is not None:
        aliases = {len(args): 0}
        in_specs.append(_any())
        args.append(y_prev)
    out_specs = [pl.BlockSpec((rows, GW), lambda s, c: (row0 + s * n_blk + c, 0))]
    out_shape = [jax.ShapeDtypeStruct((t_rows, GW), BF16)]
    for shape, blk in state_out:
        nd = len(blk)
        out_specs.append(pl.BlockSpec(blk, lambda s, c, nd=nd: (s,) + (0,) * (nd - 1)))
        out_shape.append(jax.ShapeDtypeStruct(shape, F32))
    return pl.pallas_call(
        kernel_fn,
        grid=(n_seq, n_blk),
        in_specs=in_specs,
        out_specs=out_specs,
        out_shape=out_shape,
        scratch_shapes=scratch,
        input_output_aliases=aliases,
        compiler_params=_cparams(("arbitrary", "arbitrary")),
        name=name,
    )(*args)


def _run_ssm(p, dims, l, params, buf_s, h0_s):
    t, bp, lp, bs, ls = dims
    tp = bp * lp
    q, nb, ns = TILES['ssm_q'], TILES['ssm_nb'], TILES['ssm_ns']
    cols = [(SSM_XBC, CB_XBC), (GW, CB_Z), (128, CB_DT)]
    hshape = lambda n, g: [((n, SSM_HEADS, 64, SSM_STATE), (g, SSM_HEADS, 64, SSM_STATE))]
    scratch = lambda slots, rows: [pltpu.VMEM((slots, HIST + rows, SSM_XBC), F32),
                                   pltpu.VMEM((slots, 2, SSM_STATE, 256), F32)]
    y, h_p = _mixer_call(functools.partial(_ssm_kernel, Q=q, NS=nb, has_state=False), "ssm_prompt", p, t, bp,
                         lp // (nb * q), nb * q, 0, cols, [], params, None, hshape(bp, 1), scratch(1, q), layer=l)
    y, h_s = _mixer_call(functools.partial(_ssm_kernel, Q=ls, NS=ns, has_state=True), "ssm_sample", p, t, bs // ns, 1,
                         ns * ls, tp // (ns * ls), cols,
                         [(buf_s, (ns, HIST, SSM_XBC)), (h0_s, (ns, SSM_HEADS, 64, SSM_STATE))],
                         params, y, hshape(bs, ns), scratch(ns, ls), layer=l)
    return y, h_p, h_s


def _run_lru(p, dims, l, params, buf_s, h0_s):
    t, bp, lp, bs, ls = dims
    tp = bp * lp
    r = TILES['lru_r']
    cols = [(2 * GW, CB_LRU)]
    hshape = lambda n: [((n, 1, GW), (1, 1, GW))]
    scratch = lambda rows: [pltpu.VMEM((HIST + rows, GW), F32), pltpu.VMEM((1, GW), F32)]
    y, h_p = _mixer_call(functools.partial(_lru_kernel, R=r, has_state=False), "lru_prompt", p, t, bp, lp // r, r, 0,
                         cols, [], params, None, hshape(bp), scratch(r), layer=l)
    y, h_s = _mixer_call(functools.partial(_lru_kernel, R=ls, has_state=True), "lru_sample", p, t, bs, 1, ls,
                         tp // ls, cols, [(buf_s, (1, HIST, GW)), (h0_s, (1, 1, GW))], params, y, hshape(bs),
                         scratch(ls), layer=l)
    return y, h_p[:, 0], h_s[:, 0]


def _run_swa(p, dims, l, rope_p, rope_s, sinks, k0_s, v0_s):
    t, bp, lp, bs, ls = dims
    tp = bp * lp
    q, nb, ns = TILES['swa_q'], TILES['swa_nb'], TILES['swa_ns']
    cols = [(GW, CB_Q), (256, CB_KV)]
    cshape = lambda n, g: [((n, WINDOW, 128), (g, WINDOW, 128))] * 2
    scratch = lambda g: [pltpu.VMEM((g, WINDOW, 128), F32), pltpu.VMEM((g, WINDOW, 128), F32)]
    y, k_p, v_p = _mixer_call(functools.partial(_swa_kernel, QB=q, NS=nb, has_state=False), "swa_prompt", p, t, bp,
                              lp // (nb * q), nb * q, 0, cols, [], [sinks], None, cshape(bp, 1), scratch(1),
                              tables=[(tab, (nb * q, GW)) for tab in rope_p], layer=l)
    y, k_s, v_s = _mixer_call(functools.partial(_swa_kernel, QB=ls, NS=ns, has_state=True), "swa_sample", p, t,
                              bs // ns, 1, ns * ls, tp // (ns * ls), cols,
                              [(k0_s, (ns, WINDOW, 128)), (v0_s, (ns, WINDOW, 128))], [sinks], y, cshape(bs, ns),
                              scratch(ns), tables=[(tab, (ls, GW)) for tab in rope_s], layer=l)
    return y, k_p, v_p, k_s, v_s


def _run_rwkv(p, dims, l, params, sh_rows_s, s0_s):
    t, bp, lp, bs, ls = dims
    tp = bp * lp
    c, nsub = TILES['rwkv_c'], TILES['rwkv_sub']
    g, nsub_s = TILES['rwkv_g'], TILES['rwkv_sub_s']
    cols = [(GW, CB_R), (GW, CB_K), (GW, CB_V), (256, CB_LORA)]
    sshape = lambda n, blk: [((n, 8, 64, 64), (blk, 8, 64, 64))]
    scratch = lambda slots: [pltpu.VMEM((slots, 2, 256, 256), F32), pltpu.VMEM((HIST, RWKV_COLS), F32)]
    rows_p = nsub * c
    y, s_p = _mixer_call(functools.partial(_rwkv_kernel, NSUB=nsub, G=1, C=c, has_state=False), "rwkv_prompt", p, t,
                         bp, lp // rows_p, rows_p, 0, cols, [], params, None, sshape(bp, 1), scratch(1), layer=l)
    nseq = nsub_s * g
    rows_s = nseq * ls
    y, s_s = _mixer_call(functools.partial(_rwkv_kernel, NSUB=nsub_s, G=g, C=ls, has_state=True), "rwkv_sample", p, t,
                         bs // nseq, 1, rows_s, tp // rows_s, cols,
                         [(sh_rows_s, (rows_s, RWKV_COLS)), (s0_s, (nseq, 8, 64, 64))], params, y, sshape(bs, nseq),
                         scratch(nseq), layer=l)
    return y, s_p, s_s


def _rope_tables(pos):
    half = ROT_DIM // 2
    inv = ROPE_THETA ** (-(jnp.arange(half, dtype=F32) * 2.0) / ROT_DIM)
    ang = pos.astype(F32)[:, None] * inv
    cos, sin = jnp.cos(ang), jnp.sin(ang)
    n = pos.shape[0]
    zeros = jnp.zeros((n, half), F32)
    rest = SWA_HEADDIM - ROT_DIM
    cos64 = jnp.concatenate([cos, cos, jnp.ones((n, rest), F32)], axis=1)
    sa64 = jnp.concatenate([-sin, zeros, jnp.zeros((n, rest), F32)], axis=1)
    sb64 = jnp.concatenate([zeros, sin, jnp.zeros((n, rest), F32)], axis=1)
    return tuple(jnp.tile(t, (1, GW // SWA_HEADDIM)) for t in (cos64, sa64, sb64))


def _pad_rows_front(x, axis, total):
    pad = [(0, 0)] * x.ndim
    pad[axis] = (total - x.shape[axis], 0)
    return jnp.pad(x, pad)


def kernel(x_prompt, x_sample, state_ssm, state_ssm_conv, state_rwkv, state_rwkv_shift, cache_swa_k,
           cache_swa_v, state_lru, state_lru_conv, norm_mix, w_in, ssm_conv_w, ssm_conv_b, ssm_dt_bias,
           ssm_a_log, ssm_d, ssm_norm, rwkv_mu, rwkv_w0, rwkv_w2, rwkv_a0, rwkv_a2, rwkv_g2, rwkv_k_k,
           rwkv_k_a, rwkv_r_k, rwkv_ln_w, rwkv_ln_b, swa_sinks, lru_conv_w, lru_conv_b, lru_wa, lru_ba,
           lru_wi, lru_bi, lru_lambda, w_out, norm_mlp, mlp_w1, mlp_w2, norm_final):
    bp, lp, _ = x_prompt.shape
    bs, ls, _ = x_sample.shape
    tp, ts = bp * lp, bs * ls
    t = tp + ts
    dims = (t, bp, lp, bs, ls)
    depth = w_in.shape[0]

    w_in_r = _winprep(w_in.astype(BF16))
    w_out_b = w_out.astype(BF16)
    w1_b = mlp_w1.astype(BF16)
    w2_b = mlp_w2.astype(BF16)
    eye8 = jnp.eye(8, dtype=F32)
    lru_wa_d = jnp.einsum('lncd,nm->lncmd', lru_wa, eye8).reshape(depth, GW, GW).astype(BF16)
    lru_wi_d = jnp.einsum('lncd,nm->lncmd', lru_wi, eye8).reshape(depth, GW, GW).astype(BF16)
    z64 = jnp.zeros((depth, 64, GW), F32)
    rwkv_w2_p = jnp.concatenate([rwkv_w2, z64], axis=1).astype(BF16)
    rwkv_a2_p = jnp.concatenate([z64, rwkv_a2], axis=1).astype(BF16)
    rwkv_g2_b = rwkv_g2.astype(BF16)
    dtb_p = jnp.pad(ssm_dt_bias, ((0, 0), (0, 128 - SSM_HEADS)))
    alog_p = jnp.pad(ssm_a_log, ((0, 0), (0, 128 - SSM_HEADS)))
    d_e = jnp.repeat(ssm_d, 64, axis=-1)
    sinks_b = jnp.broadcast_to(swa_sinks[:, :, None], (depth, 8, 128))
    rope_p = _rope_tables(jnp.arange(lp))
    rope_s = _rope_tables(PAST_LEN + jnp.arange(ls))

    def r2(a, l):
        return a[l][None, :]

    def rows(a):
        return a[:, None, :]

    ssm_params = [ssm_conv_w, rows(ssm_conv_b), rows(dtb_p), rows(alog_p), rows(d_e), rows(ssm_norm)]
    rwkv_params = [rows(rwkv_mu), rows(rwkv_w0), rwkv_w2_p, rows(rwkv_a0), rwkv_a2_p, rwkv_g2_b, rows(rwkv_k_k),
                   rows(rwkv_k_a), rwkv_r_k.reshape(depth, 1, GW), rows(rwkv_ln_w), rows(rwkv_ln_b)]
    lru_params = [lru_conv_w, rows(lru_conv_b), lru_wa_d, rows(lru_ba), lru_wi_d, rows(lru_bi), rows(lru_lambda)]

    ssm_buf = _pad_rows_front(state_ssm_conv, 2, HIST)
    lru_buf = _pad_rows_front(state_lru_conv, 2, HIST)
    lru_h0 = state_lru[:, :, None, :]
    sh_rows = jnp.pad(state_rwkv_shift[:, :, None, :], ((0, 0), (0, 0), (0, ls - 1), (0, 0))).reshape(depth, ts, RWKV_COLS)
    swa_k0 = cache_swa_k.reshape(depth, bs, WINDOW, 128)
    swa_v0 = cache_swa_v.reshape(depth, bs, WINDOW, 128)

    x = jnp.concatenate([x_prompt.reshape(tp, D_MODEL), x_sample.reshape(ts, D_MODEL)], axis=0)

    names = ('ssm_p', 'ssm_s', 'conv_p', 'conv_s', 'rwkv_p', 'rwkv_s', 'shift_p', 'shift_s',
             'k_p', 'k_s', 'v_p', 'v_s', 'lru_p', 'lru_s', 'lconv_p', 'lconv_s')
    outs = {n: [] for n in names}
    for l in range(depth):
        p = _inproj(x, r2(norm_mix, l), w_in_r, l)

        y_ssm, hs_p, hs_s = _run_ssm(p, dims, l, ssm_params, ssm_buf, state_ssm)
        y_rwkv, s_p, s_s = _run_rwkv(p, dims, l, rwkv_params, sh_rows, state_rwkv)
        y_swa, k_p, v_p, k_s, v_s = _run_swa(p, dims, l, rope_p, rope_s, sinks_b, swa_k0, swa_v0)
        y_lru, lh_p, lh_s = _run_lru(p, dims, l, lru_params, lru_buf, lru_h0)

        x1 = _outproj([y_ssm, y_rwkv, y_swa, y_lru], w_out_b, l, x)
        x = _mlp(x1, r2(norm_mlp, l), w1_b, w2_b, l, norm_final[None, :], l + 1 == depth)

        def tail_p(n, c0, c1):
            return jnp.stack([p[b * lp + lp - n:b * lp + lp, c0:c1] for b in range(bp)], axis=0)

        ps = p[tp:].reshape(bs, ls, PW)
        lora0 = CB_LORA * 256
        outs['ssm_p'].append(hs_p)
        outs['ssm_s'].append(hs_s)
        outs['conv_p'].append(tail_p(3, 0, SSM_XBC))
        outs['conv_s'].append(ps[:, ls - 3:, 0:SSM_XBC])
        outs['rwkv_p'].append(s_p)
        outs['rwkv_s'].append(s_s)
        outs['shift_p'].append(jnp.concatenate([tail_p(1, 5 * GW, 8 * GW), tail_p(1, lora0, lora0 + 256)], axis=-1)[:, 0])
        outs['shift_s'].append(jnp.concatenate([ps[:, ls - 1, 5 * GW:8 * GW], ps[:, ls - 1, lora0:lora0 + 256]], axis=-1))
        outs['k_p'].append(k_p.reshape(bp, WINDOW, 2, SWA_HEADDIM))
        outs['k_s'].append(k_s.reshape(bs, WINDOW, 2, SWA_HEADDIM))
        outs['v_p'].append(v_p.reshape(bp, WINDOW, 2, SWA_HEADDIM))
        outs['v_s'].append(v_s.reshape(bs, WINDOW, 2, SWA_HEADDIM))
        outs['lru_p'].append(lh_p)
        outs['lru_s'].append(lh_s)
        outs['lconv_p'].append(tail_p(3, SSM_XBC, SSM_XBC + GW))
        outs['lconv_s'].append(ps[:, ls - 3:, SSM_XBC:SSM_XBC + GW])

    st = {n: jnp.stack(v, axis=0) for n, v in outs.items()}
    return (x[:tp].reshape(bp, lp, D_MODEL), x[tp:].reshape(bs, ls, D_MODEL),
            st['ssm_p'], st['ssm_s'], st['conv_p'], st['conv_s'], st['rwkv_p'], st['rwkv_s'],
            st['shift_p'], st['shift_s'], st['k_p'], st['k_s'], st['v_p'], st['v_s'],
            st['lru_p'], st['lru_s'], st['lconv_p'], st['lconv_s'])
```

```python
import functools
import math

import jax
import jax.numpy as jnp
from jax import lax
from jax.experimental import pallas as pl
from jax.experimental.pallas import tpu as pltpu

F32 = jnp.float32
BF16 = jnp.bfloat16

D_MODEL = 2048
GW = 512
RMS_EPS = 1e-6
CONV_W = 4
SSM_HEADS = 8
SSM_STATE = 128
SSM_XBC = 1024
RWKV_COLS = 1792
RWKV_GN_EPS = 64e-5
SWA_HEADDIM = 64
WINDOW = 128
ROT_DIM = 16
ROPE_THETA = 500000.0
LRU_C = 8.0
D_FF = 4 * D_MODEL
PAST_LEN = 16384

PW = 5376
CB_XBC, CB_LRU = 0, 1
CB_Z, CB_R, CB_K, CB_V, CB_Q = 4, 5, 6, 7, 8
CB_KV, CB_LORA = 18, 19
CB_DT = 40

HIST = 8
VMEM_LIMIT = 56 * 1024 * 1024

TILES = dict(
    prep_tk=256, in_tm=1408, in_tn=768, out_tm=704, mlp_tm=704, mlp_tf=1024, mlp_slabs=64,
    ssm_q=128, ssm_nb=4, swa_q=128, swa_nb=4, lru_r=256, rwkv_c=64, rwkv_sub=8,
    ssm_ns=4, swa_ns=8, rwkv_g=4, rwkv_sub_s=2,
)


def _cparams(sem):
    return pltpu.CompilerParams(dimension_semantics=sem, vmem_limit_bytes=VMEM_LIMIT)


def _mm(a, b):
    return jnp.dot(a.astype(BF16), b.astype(BF16), preferred_element_type=F32)


def _mm_nt(a, b):
    return lax.dot_general(a.astype(BF16), b.astype(BF16), (((1,), (1,)), ((), ())),
                           preferred_element_type=F32)


def _mm_tn(a, b):
    return lax.dot_general(a.astype(BF16), b.astype(BF16), (((0,), (0,)), ((), ())),
                           preferred_element_type=F32)


def _split3(x):
    x1 = x.astype(BF16)
    r = x - x1.astype(F32)
    x2 = r.astype(BF16)
    r = r - x2.astype(F32)
    return x1, x2, r.astype(BF16)


def _mm_exact_rhs(a, sel):
    p1, p2, p3 = _split3(a)
    return _mm(p1, sel) + _mm(p2, sel) + _mm(p3, sel)


def _mm_split2_rhs(a, sel):
    p1 = a.astype(BF16)
    p2 = (a - p1.astype(F32)).astype(BF16)
    return _mm(p1, sel) + _mm(p2, sel)


def _mm_exact_lhs(sel, b):
    p1, p2, p3 = _split3(b)
    return _mm(sel, p1) + _mm(sel, p2) + _mm(sel, p3)


def _mm_nt_exact_lhs(sel, b):
    p1, p2, p3 = _split3(b)
    return _mm_nt(sel, p1) + _mm_nt(sel, p2) + _mm_nt(sel, p3)


def _softplus(x):
    return jnp.maximum(x, 0.0) + jnp.log1p(jnp.exp(-jnp.abs(x)))


def _sigmoid(x):
    return 1.0 / (1.0 + jnp.exp(-x))


def _iota(shape, dim):
    return lax.broadcasted_iota(jnp.int32, shape, dim)


def _ind(mask):
    return jnp.where(mask, 1.0, 0.0)


def _rms(x, g):
    ms = jnp.mean(x * x, axis=-1, keepdims=True)
    return x * lax.rsqrt(ms + RMS_EPS) * g


def _conv_from_hist(ext_ref, rows, cw_ref, cb_ref):
    acc = cb_ref[...] + cw_ref[0:1, :] * ext_ref[HIST - 3:HIST - 3 + rows, :]
    for j in range(1, CONV_W):
        acc = acc + cw_ref[j:j + 1, :] * ext_ref[HIST - 3 + j:HIST - 3 + j + rows, :]
    return acc


def _winprep_kernel(w_ref, o_ref):
    o_rwkv = GW + SSM_XBC + SSM_HEADS
    o_swa = o_rwkv + RWKV_COLS
    o_lru = o_swa + GW + 256
    sections = ((GW, SSM_XBC), (o_lru, 2 * GW), (0, GW), (o_rwkv, 3 * GW), (o_swa, GW), (o_swa + GW, 256),
                (o_rwkv + 3 * GW, 256), (GW + SSM_XBC, SSM_HEADS))
    dst = 0
    for src, width in sections:
        o_ref[:, dst:dst + width] = w_ref[:, src:src + width]
        dst += width
    o_ref[:, dst:PW] = jnp.zeros((o_ref.shape[0], PW - dst), BF16)


def _inproj_kernel(x_ref, g_ref, w_ref, o_ref, h_ref):
    @pl.when(pl.program_id(1) == 0)
    def _():
        h_ref[...] = _rms(x_ref[...], g_ref[...]).astype(BF16)

    o_ref[...] = jnp.dot(h_ref[...], w_ref[...], preferred_element_type=F32)


def _outproj_kernel(y0_ref, y1_ref, y2_ref, y3_ref, w_ref, x_ref, x1_ref):
    acc = x_ref[...]
    for i, y_ref in enumerate((y0_ref, y1_ref, y2_ref, y3_ref)):
        acc = acc + jnp.dot(y_ref[...], w_ref[i * GW:(i + 1) * GW, :], preferred_element_type=F32)
    x1_ref[...] = acc


def _mlp_kernel(*refs, final, cast_next):
    if cast_next:
        x_ref, g_ref, w1_ref, w2_ref, gf_ref, w1n_ref, w2n_ref, xo_ref, w1o_ref, w2o_ref, h_ref = refs
        w1o_ref[...] = w1n_ref[...].astype(BF16)
        w2o_ref[...] = w2n_ref[...].astype(BF16)
    else:
        x_ref, g_ref, w1_ref, w2_ref, gf_ref, xo_ref, h_ref = refs
    k = pl.program_id(1)

    @pl.when(k == 0)
    def _():
        x = x_ref[...]
        h_ref[...] = _rms(x, g_ref[...]).astype(BF16)
        xo_ref[...] = x

    a = jnp.dot(h_ref[...], w1_ref[...], preferred_element_type=F32)
    a = jnp.square(jnp.maximum(a, 0.0)).astype(BF16)
    xo_ref[...] += jnp.dot(a, w2_ref[...], preferred_element_type=F32)

    if final:
        @pl.when(k == pl.num_programs(1) - 1)
        def _():
            xo_ref[...] = _rms(xo_ref[...], gf_ref[...])


def _ssm_kernel(*refs, Q, NS, has_state):
    if has_state:
        (xbc_ref, z_ref, dt_ref, buf_ref, h0_ref, cw_ref, cb_ref, dtb_ref, alog_ref, de_ref, nw_ref,
         _, y_ref, hout_ref, ext_ref, h_ref) = refs
    else:
        (xbc_ref, z_ref, dt_ref, cw_ref, cb_ref, dtb_ref, alog_ref, de_ref, nw_ref,
         y_ref, hout_ref, ext_ref, h_ref) = refs
    c = pl.program_id(1)
    n_slots = NS if has_state else 1

    @pl.when(c == 0)
    def _():
        if has_state:
            for slot in range(n_slots):
                ext_ref[slot, 0:HIST, :] = buf_ref[slot]
                for g in range(2):
                    h_ref[slot, g] = h0_ref[slot, 4 * g:4 * g + 4].reshape(256, SSM_STATE).T
        else:
            ext_ref[0, 0:HIST, :] = jnp.zeros((HIST, SSM_XBC), F32)
            h_ref[...] = jnp.zeros(h_ref.shape, F32)

    a_neg = jnp.where(_iota((1, 128), 1) < SSM_HEADS, -jnp.exp(alog_ref[...]), 0.0)
    tri = (_iota((Q, Q), 0) >= _iota((Q, Q), 1))
    tri16 = _ind(tri).astype(BF16)
    expand = _ind(_iota((128, GW), 0) == (_iota((128, GW), 1) >> 6)).astype(BF16)
    sel = _ind(_iota((SSM_HEADS, 128), 0) == _iota((SSM_HEADS, 128), 1)).astype(BF16)
    lanehead = _iota((1, 256), 1) >> 6

    for s in range(NS):
        slot = s if has_state else 0
        rows = slice(s * Q, (s + 1) * Q)
        ext = ext_ref.at[slot]
        ext[HIST:HIST + Q, :] = xbc_ref[rows, :]
        pre = _conv_from_hist(ext, Q, cw_ref, cb_ref)
        ext[0:HIST, :] = ext[Q:Q + HIST, :]
        xbc = pre * _sigmoid(pre)
        x = xbc[:, 0:GW]
        bm = xbc[:, GW:GW + 256]
        cm = xbc[:, GW + 256:GW + 512]

        dt = _softplus(dt_ref[rows, :] + dtb_ref[...])
        cum = _mm_exact_lhs(tri16, dt * a_neg)
        dt_e = _mm_exact_rhs(dt, expand)
        cum_e = _mm_exact_rhs(cum, expand)
        cum_t = _mm_nt_exact_lhs(sel, cum)
        last = cum_e[Q - 1:Q, :]
        expc_e = jnp.exp(cum_e)
        tail_e = jnp.exp(last - cum_e)
        dtot_e = jnp.exp(last)
        dx = dt_e * x

        ys = []
        for g in range(2):
            bg = bm[:, g * 128:(g + 1) * 128]
            cg = cm[:, g * 128:(g + 1) * 128]
            cb = _mm_nt(cg, bg)
            h_t = h_ref[slot, g]
            yg = _mm(cg, h_t) * expc_e[:, g * 256:(g + 1) * 256]
            dxg = dx[:, g * 256:(g + 1) * 256]
            for hh in range(4):
                h = 4 * g + hh
                decay = jnp.exp(jnp.minimum(cum[:, h:h + 1] - cum_t[h:h + 1, :], 0.0))
                m = jnp.where(tri, cb * decay, 0.0)
                yg = yg + _mm(m, jnp.where(lanehead == hh, dxg, 0.0))
            ys.append(yg)
            h_ref[slot, g] = (h_t * dtot_e[:, g * 256:(g + 1) * 256]
                              + _mm_tn(bg, dxg * tail_e[:, g * 256:(g + 1) * 256]))

        y = jnp.concatenate(ys, axis=1) + de_ref[...] * x
        z = z_ref[rows, :]
        y = y * (z * _sigmoid(z))
        outs = []
        for g in range(2):
            yg = y[:, g * 256:(g + 1) * 256]
            outs.append(yg * lax.rsqrt(jnp.mean(yg * yg, axis=-1, keepdims=True) + RMS_EPS))
        y_ref[rows, :] = (jnp.concatenate(outs, axis=1) * nw_ref[...]).astype(y_ref.dtype)

    @pl.when(c == pl.num_programs(1) - 1)
    def _():
        for slot in range(n_slots):
            for g in range(2):
                hout_ref[slot, 4 * g:4 * g + 4] = h_ref[slot, g].T.reshape(4, 64, SSM_STATE)


def _lru_kernel(*refs, R, has_state):
    if has_state:
        (u_ref, buf_ref, h0_ref, cw_ref, cb_ref, wa_ref, ba_ref, wi_ref, bi_ref, lam_ref,
         _, y_ref, hout_ref, ext_ref, h_ref) = refs
    else:
        (u_ref, cw_ref, cb_ref, wa_ref, ba_ref, wi_ref, bi_ref, lam_ref,
         y_ref, hout_ref, ext_ref, h_ref) = refs
    c = pl.program_id(1)

    @pl.when(c == 0)
    def _():
        if has_state:
            ext_ref[0:HIST, :] = buf_ref[0]
            h_ref[...] = h0_ref[0]
        else:
            ext_ref[0:HIST, :] = jnp.zeros((HIST, GW), F32)
            h_ref[...] = jnp.zeros(h_ref.shape, F32)

    u = u_ref[...]
    gate = u[:, GW:]
    ext_ref[HIST:HIST + R, :] = u[:, :GW]
    xc = _conv_from_hist(ext_ref, R, cw_ref, cb_ref)
    ext_ref[0:HIST, :] = ext_ref[R:R + HIST, :]

    rg = _sigmoid(_mm(xc, wa_ref[...]) + ba_ref[...])
    ig = _sigmoid(_mm(xc, wi_ref[...]) + bi_ref[...])
    log_a = -LRU_C * rg * _softplus(-lam_ref[...])
    a = jnp.exp(log_a)
    th = jnp.tanh(log_a)
    b = jnp.sqrt(-2.0 * th / (1.0 - th)) * (ig * xc)

    row = _iota((R, 1), 0)
    s = 1
    while s < R:
        keep = row >= s
        a_prev = jnp.where(keep, pltpu.roll(a, s, 0), 1.0)
        b_prev = jnp.where(keep, pltpu.roll(b, s, 0), 0.0)
        b = b + a * b_prev
        a = a * a_prev
        s *= 2
    h = b + a * h_ref[...]
    h_ref[...] = h[R - 1:R, :]
    hout_ref[0] = h[R - 1:R, :]

    gelu = 0.5 * gate * (1.0 + jnp.tanh(math.sqrt(2.0 / math.pi) * (gate + 0.044715 * (gate * gate * gate))))
    y_ref[...] = (h * gelu).astype(y_ref.dtype)


def _swa_kernel(*refs, QB, NS, has_state):
    if has_state:
        (q_ref, kv_ref, kc_ref, vc_ref, cos_ref, sa_ref, sb_ref, sink_ref,
         _, y_ref, kout_ref, vout_ref, hk_ref, hv_ref) = refs
    else:
        (q_ref, kv_ref, cos_ref, sa_ref, sb_ref, sink_ref,
         y_ref, kout_ref, vout_ref, hk_ref, hv_ref) = refs
    c = pl.program_id(1)
    nk = WINDOW + QB
    lq = QB.bit_length() - 1

    @pl.when(c == 0)
    def _():
        if has_state:
            hk_ref[...] = kc_ref[...]
            hv_ref[...] = vc_ref[...]
        else:
            hk_ref[...] = jnp.zeros(hk_ref.shape, F32)
            hv_ref[...] = jnp.zeros(hv_ref.shape, F32)

    def rope(x, width, trows):
        cos, sa, sb = cos_ref[trows, 0:width], sa_ref[trows, 0:width], sb_ref[trows, 0:width]
        return x * cos + pltpu.roll(x, width - ROT_DIM // 2, 1) * sa + pltpu.roll(x, ROT_DIM // 2, 1) * sb

    kj = _iota((nk, 4 * QB), 0)
    qi = _iota((nk, 4 * QB), 1) & (QB - 1)
    neg_mask = jnp.where(kj >= qi, jnp.where(kj <= qi + WINDOW, 0.0, -jnp.inf), -jnp.inf)
    neg_mask0 = neg_mask if has_state else jnp.where(c == 0, jnp.where(kj >= WINDOW, neg_mask, -jnp.inf), neg_mask)
    upper = _iota((1, 128), 1) >= SWA_HEADDIM
    lower = _iota((1, 128), 1) < SWA_HEADDIM
    chead = _iota((1, 4 * QB), 1) >> lq
    sink_rows = []
    for g in range(2):
        sr = sink_ref[4 * g + 3:4 * g + 4, 0:1]
        for hh in (2, 1, 0):
            sr = jnp.where(chead == hh, sink_ref[4 * g + hh:4 * g + hh + 1, 0:1], sr)
        sink_rows.append(sr)

    assert has_state or QB == WINDOW
    qs, keys, vals = [], [], []
    for s in range(NS):
        rows = slice(s * QB, (s + 1) * QB)
        trows = slice(0, QB) if has_state else rows
        qs.append(rope(q_ref[rows, :], GW, trows) * (SWA_HEADDIM ** -0.5))
        kv = kv_ref[rows, :]
        k_new = rope(kv[:, 0:128], 128, trows)
        v_new = kv[:, 128:256]
        if has_state or s == 0:
            hist_k, hist_v = hk_ref[s], hv_ref[s]
        else:
            hist_k, hist_v = keys[s - 1][WINDOW:, :], vals[s - 1][WINDOW:, :]
        keys.append(jnp.concatenate([hist_k, k_new], axis=0))
        vals.append(jnp.concatenate([hist_v, v_new], axis=0))

    pairs = [(s, g) for s in range(NS) for g in range(2)]
    vds, scores = [], []
    for s, g in pairs:
        own = upper if g == 1 else lower
        kd = jnp.where(own, keys[s], pltpu.roll(keys[s], SWA_HEADDIM, 1))
        vds.append(jnp.where(own, vals[s], pltpu.roll(vals[s], SWA_HEADDIM, 1)))
        parts = []
        for hh in range(4):
            h = 4 * g + hh
            mine = upper if h % 2 == 1 else lower
            parts.append(jnp.where(mine, qs[s][:, 128 * (h // 2):128 * (h // 2 + 1)], 0.0))
        mask = neg_mask0 if s == 0 else neg_mask
        scores.append(_mm_nt(kd, jnp.concatenate(parts, axis=0)) + mask)
    probs = []
    for (s, g), sc in zip(pairs, scores):
        m = jnp.maximum(jnp.max(sc, axis=0, keepdims=True), sink_rows[g])
        e = jnp.exp(sc - m)
        den = jnp.sum(e, axis=0, keepdims=True) + jnp.exp(sink_rows[g] - m)
        probs.append(e / den)
    outs = [_mm_tn(pr, vd) for pr, vd in zip(probs, vds)]

    for s in range(NS):
        tiles = [None] * 4
        for g in range(2):
            o = outs[2 * s + g]
            for hh in range(4):
                h = 4 * g + hh
                oh = o[hh * QB:(hh + 1) * QB]
                tiles[h // 2] = oh if h % 2 == 0 else jnp.where(upper, oh, tiles[h // 2])
        y_ref[s * QB:(s + 1) * QB, :] = jnp.concatenate(tiles, axis=1).astype(y_ref.dtype)
        if has_state:
            hk_ref[s] = keys[s][QB:, :]
            hv_ref[s] = vals[s][QB:, :]
    if not has_state:
        hk_ref[0] = keys[NS - 1][WINDOW:, :]
        hv_ref[0] = vals[NS - 1][WINDOW:, :]
    kout_ref[...] = hk_ref[...]
    vout_ref[...] = hv_ref[...]


def _seg_sum(x):
    ones = _ind((_iota((256, 256), 0) >> 6) == (_iota((256, 256), 1) >> 6)).astype(BF16)
    return jnp.concatenate([_mm_split2_rhs(x[:, 0:256], ones), _mm_split2_rhs(x[:, 256:512], ones)], axis=1)


def _rwkv_kernel(*refs, NSUB, G, C, has_state):
    if has_state:
        (r_ref, k_ref, v_ref, lo_ref, sh_ref, h0_ref, mu_ref, w0_ref, w2_ref, a0_ref, a2_ref, g2_ref,
         kk_ref, ka_ref, rk_ref, lnw_ref, lnb_ref, _, y_ref, hout_ref, hs_ref, carry_ref) = refs
    else:
        (r_ref, k_ref, v_ref, lo_ref, mu_ref, w0_ref, w2_ref, a0_ref, a2_ref, g2_ref,
         kk_ref, ka_ref, rk_ref, lnw_ref, lnb_ref, y_ref, hout_ref, hs_ref, carry_ref) = refs
    c = pl.program_id(1)
    R = G * C
    RB = NSUB * R
    lc = C.bit_length() - 1
    n_slots = NSUB * G if has_state else 1

    @pl.when(c == 0)
    def _():
        if has_state:
            for slot in range(n_slots):
                for hh in range(2):
                    rows = [jnp.pad(h0_ref[slot, 4 * hh + h], ((0, 0), (64 * h, 192 - 64 * h))) for h in range(4)]
                    hs_ref[slot, hh] = jnp.concatenate(rows, axis=0)
        else:
            hs_ref[...] = jnp.zeros(hs_ref.shape, F32)
            carry_ref[...] = jnp.zeros(carry_ref.shape, F32)

    row = _iota((RB, 1), 0)
    first = ((row & (C - 1)) == 0) if has_state else (row == 0)
    mixed = []
    for ref, off, width in ((r_ref, 0, GW), (k_ref, GW, GW), (v_ref, 2 * GW, GW), (lo_ref, 3 * GW, 256)):
        pf = ref[...]
        if has_state:
            fv = sh_ref[:, off:off + width]
        else:
            fv = carry_ref[0:1, off:off + width]
        shifted = jnp.where(first, fv, pltpu.roll(pf, 1, 0))
        mixed.append(pf + (shifted - pf) * mu_ref[:, off:off + width])
        if not has_state:
            carry_ref[0:1, off:off + width] = pf[RB - 1:RB, :]
    r, k, v, lo = mixed

    lo_wa = lo[:, 0:128]
    w_log = -_softplus(-(w0_ref[...] + _mm(jnp.tanh(lo_wa), w2_ref[...]))) - 0.5
    lw = -jnp.exp(w_log)
    a = _sigmoid(a0_ref[...] + _mm(lo_wa, a2_ref[...]))
    gate = _mm(_sigmoid(lo[:, 128:256]), g2_ref[...])
    kk = k * kk_ref[...]
    kmod = k * (1.0 + (a - 1.0) * ka_ref[...])
    kk = kk * lax.rsqrt(jnp.maximum(_seg_sum(kk * kk), 1e-24))
    kb = kk * a

    NB = 4 * R
    lanehead = _iota((1, 256), 1) >> 6
    bi, bj = _iota((NB, NB), 0), _iota((NB, NB), 1)
    same = _ind((bi >> lc) == (bj >> lc))
    strict = same * _ind(bj < bi)
    incl = same * _ind(bj <= bi)
    eye_nb = _ind(bi == bj)
    si, sj = _iota((R, R), 0), _iota((R, R), 1)
    seg = _ind((si >> lc) == (sj >> lc))
    seg_ones = seg.astype(BF16)
    seg_tri = (seg * _ind(sj <= si)).astype(BF16)

    def bd(x):
        parts = []
        for s in range(G):
            xs = x[s * C:(s + 1) * C]
            for h in range(4):
                parts.append(jnp.where(lanehead == h, xs, 0.0))
        return jnp.concatenate(parts, axis=0)

    chains = [(sub, hh) for sub in range(NSUB) for hh in range(2)]
    ops = []
    for sub, hh in chains:
        sr, sl = slice(sub * R, (sub + 1) * R), slice(hh * 256, (hh + 1) * 256)
        lw4 = lw[sr, sl]
        clw = _mm_exact_lhs(seg_tri, lw4)
        clt = clw[C - 1:C, :] if G == 1 else _mm_exact_lhs(seg_ones, lw4)
        e_out = jnp.exp(-clw)
        e_end = jnp.exp(clt - clw)
        b_rq = bd(r[sr, sl] * jnp.exp(clw))
        ops.append(dict(
            kq=bd(kk[sr, sl] * jnp.exp(clw - lw4)).astype(BF16), rq=b_rq, rq16=b_rq.astype(BF16),
            kk=bd(kmod[sr, sl] * e_out).astype(BF16), bk=bd(kb[sr, sl] * e_out).astype(BF16),
            kk2=bd(kmod[sr, sl] * e_end).astype(BF16), bk2=bd(kb[sr, sl] * e_end).astype(BF16),
            v=bd(v[sr, sl]).astype(BF16), p_end=jnp.exp(clt)))

    n_mat = [strict * _mm_nt(o['kq'], o['bk']) for o in ops]
    a_kk = [strict * _mm_nt(o['kq'], o['kk']) for o in ops]
    a_rb = [incl * _mm_nt(o['rq16'], o['bk']) for o in ops]
    a_rk = [incl * _mm_nt(o['rq16'], o['kk']) for o in ops]

    t_inv = [eye_nb - n for n in n_mat]
    pw = n_mat
    for _ in range(lc - 1):
        pw = [_mm(x, x) for x in pw]
        t_inv = [t + _mm(t, x) for t, x in zip(t_inv, pw)]

    av = [_mm(a, o['v']) for a, o in zip(a_kk, ops)]
    wu = [_mm(t, jnp.concatenate([o['kq'], x.astype(BF16)], axis=1)) for t, o, x in zip(t_inv, ops, av)]
    arb_wu = [_mm(a, x) for a, x in zip(a_rb, wu)]
    rq2 = [o['rq'] - x[:, 0:256] for o, x in zip(ops, arb_wu)]
    y_loc = [_mm(a, o['v']) - x[:, 256:512] for a, o, x in zip(a_rk, ops, arb_wu)]
    wu16 = [x.astype(BF16) for x in wu]

    upd = []
    for i, o in enumerate(ops):
        per_seg = []
        for s in range(G):
            rs = slice(s * 4 * C, (s + 1) * 4 * C)
            bwq = _mm_tn(o['bk2'][rs], wu16[i][rs, 0:256])
            add = _mm_tn(o['v'][rs], o['kk2'][rs]) - _mm_tn(wu16[i][rs, 256:512], o['bk2'][rs])
            per_seg.append((bwq.astype(BF16), add))
        upd.append(per_seg)

    y_parts = {}
    for i, (sub, hh) in enumerate(chains):
        ys = []
        for s in range(G):
            rs = slice(s * 4 * C, (s + 1) * 4 * C)
            slot = sub * G + s if has_state else 0
            g0 = hs_ref[slot, hh]
            g0b = g0.astype(BF16)
            yb = _mm_nt(rq2[i][rs], g0b) + y_loc[i][rs]
            y4 = yb[0:C]
            for h in range(1, 4):
                y4 = y4 + yb[h * C:(h + 1) * C]
            ys.append(y4)
            bwq, add = upd[i][s]
            hs_ref[slot, hh] = g0 * ops[i]['p_end'][s * C:s * C + 1, :] - _mm_nt(g0b, bwq) + add
        y_parts[(sub, hh)] = ys[0] if G == 1 else jnp.concatenate(ys, axis=0)
    y_subs = [jnp.concatenate([y_parts[(sub, 0)], y_parts[(sub, 1)]], axis=1) for sub in range(NSUB)]
    y = y_subs[0] if NSUB == 1 else jnp.concatenate(y_subs, axis=0)

    mean = _seg_sum(y) * (1.0 / 64.0)
    yc = y - mean
    var = _seg_sum(yc * yc) * (1.0 / 64.0)
    yn = yc * lax.rsqrt(var + RWKV_GN_EPS) * lnw_ref[...] + lnb_ref[...]
    bonus = _seg_sum(r * kmod * rk_ref[...]) * v
    y_ref[...] = ((yn + bonus) * gate).astype(y_ref.dtype)

    @pl.when(c == pl.num_programs(1) - 1)
    def _():
        for slot in range(n_slots):
            for hh in range(2):
                for h in range(4):
                    hout_ref[slot, 4 * hh + h] = hs_ref[slot, hh, 64 * h:64 * h + 64, 64 * h:64 * h + 64]


def _row(n):
    return pl.BlockSpec((1, n), lambda *_: (0, 0))


def _full(shape):
    nd = len(shape)
    return pl.BlockSpec(shape, lambda *_: (0,) * nd)


def _any():
    return pl.BlockSpec(memory_space=pl.ANY)


def _winprep(w_in):
    depth, d, cols = w_in.shape
    tk = TILES['prep_tk']
    return pl.pallas_call(
        _winprep_kernel,
        grid=(depth, d // tk),
        in_specs=[pl.BlockSpec((None, tk, cols), lambda l, i: (l, i, 0))],
        out_specs=pl.BlockSpec((None, tk, PW), lambda l, i: (l, i, 0)),
        out_shape=jax.ShapeDtypeStruct((depth, d, PW), BF16),
        compiler_params=_cparams(("parallel", "parallel")),
        name="winprep",
    )(w_in)


def _inproj(x, g, w_all, l):
    t, tm, tn = x.shape[0], TILES['in_tm'], TILES['in_tn']
    return pl.pallas_call(
        _inproj_kernel,
        grid=(t // tm, PW // tn),
        in_specs=[pl.BlockSpec((tm, D_MODEL), lambda i, j: (i, 0)), _row(D_MODEL),
                  pl.BlockSpec((None, D_MODEL, tn), lambda i, j: (l, 0, j))],
        out_specs=pl.BlockSpec((tm, tn), lambda i, j: (i, j)),
        out_shape=jax.ShapeDtypeStruct((t, PW), F32),
        scratch_shapes=[pltpu.VMEM((tm, D_MODEL), BF16)],
        compiler_params=_cparams(("parallel", "arbitrary")),
        name="inproj",
    )(x, g, w_all)


def _outproj(ys, w_all, l, x):
    t, tm = x.shape[0], TILES['out_tm']
    yspec = pl.BlockSpec((tm, GW), lambda i: (i, 0))
    xspec = pl.BlockSpec((tm, D_MODEL), lambda i: (i, 0))
    return pl.pallas_call(
        _outproj_kernel,
        grid=(t // tm,),
        in_specs=[yspec, yspec, yspec, yspec,
                  pl.BlockSpec((None, D_MODEL, D_MODEL), lambda i: (l, 0, 0)), xspec],
        out_specs=xspec,
        out_shape=jax.ShapeDtypeStruct((t, D_MODEL), F32),
        compiler_params=_cparams(("parallel",)),
        name="outproj",
    )(*ys, w_all, x)


def _mlp(x, g, w1, w2, g_final, final, w1_f32, w2_f32, nxt):
    t, tm, tf, ns = x.shape[0], TILES['mlp_tm'], TILES['mlp_tf'], TILES['mlp_slabs']
    nk = D_FF // tf
    xspec = pl.BlockSpec((tm, D_MODEL), lambda i, k: (i, 0))
    in_specs = [xspec, _row(D_MODEL), pl.BlockSpec((D_MODEL, tf), lambda i, k: (0, k)),
                pl.BlockSpec((tf, D_MODEL), lambda i, k: (k, 0)), _row(D_MODEL)]
    out_specs = [xspec]
    out_shape = [jax.ShapeDtypeStruct((t, D_MODEL), F32)]
    args = [x, g, w1, w2, g_final]
    if not final:
        def slab(i, k):
            return jnp.minimum(i * nk + k, ns - 1)
        r1, r2 = D_MODEL // ns, D_FF // ns
        in_specs += [pl.BlockSpec((None, r1, D_FF), lambda i, k: (nxt, slab(i, k), 0)),
                     pl.BlockSpec((None, r2, D_MODEL), lambda i, k: (nxt, slab(i, k), 0))]
        out_specs += [pl.BlockSpec((r1, D_FF), lambda i, k: (slab(i, k), 0)),
                      pl.BlockSpec((r2, D_MODEL), lambda i, k: (slab(i, k), 0))]
        out_shape += [jax.ShapeDtypeStruct((D_MODEL, D_FF), BF16), jax.ShapeDtypeStruct((D_FF, D_MODEL), BF16)]
        args += [w1_f32, w2_f32]
    return pl.pallas_call(
        functools.partial(_mlp_kernel, final=final, cast_next=not final),
        grid=(t // tm, nk),
        in_specs=in_specs,
        out_specs=out_specs,
        out_shape=out_shape,
        scratch_shapes=[pltpu.VMEM((tm, D_MODEL), BF16)],
        compiler_params=_cparams(("arbitrary", "arbitrary")),
        name="mlp",
    )(*args)


def _mixer_call(kernel_fn, name, p, t_rows, n_seq, n_blk, rows, row0, col_specs, state_in, params, y_prev,
                state_out, scratch, tables=(), layer=0):
    def rowmap(cb):
        return lambda s, c: (row0 + s * n_blk + c, cb)

    in_specs = [pl.BlockSpec((rows, w), rowmap(cb)) for w, cb in col_specs]
    args = [p] * len(col_specs)
    for arr, blk in state_in:
        nd = len(blk)
        in_specs.append(pl.BlockSpec((None,) + blk, lambda s, c, nd=nd: (layer, s) + (0,) * (nd - 1)))
        args.append(arr)
    for tab, blk in tables:
        in_specs.append(pl.BlockSpec(blk, lambda s, c: (c, 0)))
        args.append(tab)
    for arr in params:
        in_specs.append(pl.BlockSpec((None,) + arr.shape[1:], lambda s, c, nd=arr.ndim: (layer,) + (0,) * (nd - 1)))
        args.append(arr)
    aliases = {}
    if y_prev is not None:
        aliases = {len(args): 0}
        in_specs.append(_any())
        args.append(y_prev)
    out_specs = [pl.BlockSpec((rows, GW), lambda s, c: (row0 + s * n_blk + c, 0))]
    out_shape = [jax.ShapeDtypeStruct((t_rows, GW), BF16)]
    for shape, blk in state_out:
        nd = len(blk)
        out_specs.append(pl.BlockSpec(blk, lambda s, c, nd=nd: (s,) + (0,) * (nd - 1)))
        out_shape.append(jax.ShapeDtypeStruct(shape, F32))
    return pl.pallas_call(
        kernel_fn,
        grid=(n_seq, n_blk),
        in_specs=in_specs,
        out_specs=out_specs,
        out_shape=out_shape,
        scratch_shapes=scratch,
        input_output_aliases=aliases,
        compiler_params=_cparams(("arbitrary", "arbitrary")),
        name=name,
    )(*args)


def _run_ssm(p, dims, l, params, buf_s, h0_s):
    t, bp, lp, bs, ls = dims
    tp = bp * lp
    q, nb, ns = TILES['ssm_q'], TILES['ssm_nb'], TILES['ssm_ns']
    cols = [(SSM_XBC, CB_XBC), (GW, CB_Z), (128, CB_DT)]
    hshape = lambda n, g: [((n, SSM_HEADS, 64, SSM_STATE), (g, SSM_HEADS, 64, SSM_STATE))]
    scratch = lambda slots, rows: [pltpu.VMEM((slots, HIST + rows, SSM_XBC), F32),
                                   pltpu.VMEM((slots, 2, SSM_STATE, 256), F32)]
    y, h_p = _mixer_call(functools.partial(_ssm_kernel, Q=q, NS=nb, has_state=False), "ssm_prompt", p, t, bp,
                         lp // (nb * q), nb * q, 0, cols, [], params, None, hshape(bp, 1), scratch(1, q), layer=l)
    y, h_s = _mixer_call(functools.partial(_ssm_kernel, Q=ls, NS=ns, has_state=True), "ssm_sample", p, t, bs // ns, 1,
                         ns * ls, tp // (ns * ls), cols,
                         [(buf_s, (ns, HIST, SSM_XBC)), (h0_s, (ns, SSM_HEADS, 64, SSM_STATE))],
                         params, y, hshape(bs, ns), scratch(ns, ls), layer=l)
    return y, h_p, h_s


def _run_lru(p, dims, l, params, buf_s, h0_s):
    t, bp, lp, bs, ls = dims
    tp = bp * lp
    r = TILES['lru_r']
    cols = [(2 * GW, CB_LRU)]
    hshape = lambda n: [((n, 1, GW), (1, 1, GW))]
    scratch = lambda rows: [pltpu.VMEM((HIST + rows, GW), F32), pltpu.VMEM((1, GW), F32)]
    y, h_p = _mixer_call(functools.partial(_lru_kernel, R=r, has_state=False), "lru_prompt", p, t, bp, lp // r, r, 0,
                         cols, [], params, None, hshape(bp), scratch(r), layer=l)
    y, h_s = _mixer_call(functools.partial(_lru_kernel, R=ls, has_state=True), "lru_sample", p, t, bs, 1, ls,
                         tp // ls, cols, [(buf_s, (1, HIST, GW)), (h0_s, (1, 1, GW))], params, y, hshape(bs),
                         scratch(ls), layer=l)
    return y, h_p[:, 0], h_s[:, 0]


def _run_swa(p, dims, l, rope_p, rope_s, sinks, k0_s, v0_s):
    t, bp, lp, bs, ls = dims
    tp = bp * lp
    q, nb, ns = TILES['swa_q'], TILES['swa_nb'], TILES['swa_ns']
    cols = [(GW, CB_Q), (256, CB_KV)]
    cshape = lambda n, g: [((n, WINDOW, 128), (g, WINDOW, 128))] * 2
    scratch = lambda g: [pltpu.VMEM((g, WINDOW, 128), F32), pltpu.VMEM((g, WINDOW, 128), F32)]
    y, k_p, v_p = _mixer_call(functools.partial(_swa_kernel, QB=q, NS=nb, has_state=False), "swa_prompt", p, t, bp,
                              lp // (nb * q), nb * q, 0, cols, [], [sinks], None, cshape(bp, 1), scratch(1),
                              tables=[(tab, (nb * q, GW)) for tab in rope_p], layer=l)
    y, k_s, v_s = _mixer_call(functools.partial(_swa_kernel, QB=ls, NS=ns, has_state=True), "swa_sample", p, t,
                              bs // ns, 1, ns * ls, tp // (ns * ls), cols,
                              [(k0_s, (ns, WINDOW, 128)), (v0_s, (ns, WINDOW, 128))], [sinks], y, cshape(bs, ns),
                              scratch(ns), tables=[(tab, (ls, GW)) for tab in rope_s], layer=l)
    return y, k_p, v_p, k_s, v_s


def _run_rwkv(p, dims, l, params, sh_rows_s, s0_s):
    t, bp, lp, bs, ls = dims
    tp = bp * lp
    c, nsub = TILES['rwkv_c'], TILES['rwkv_sub']
    g, nsub_s = TILES['rwkv_g'], TILES['rwkv_sub_s']
    cols = [(GW, CB_R), (GW, CB_K), (GW, CB_V), (256, CB_LORA)]
    sshape = lambda n, blk: [((n, 8, 64, 64), (blk, 8, 64, 64))]
    scratch = lambda slots: [pltpu.VMEM((slots, 2, 256, 256), F32), pltpu.VMEM((HIST, RWKV_COLS), F32)]
    rows_p = nsub * c
    y, s_p = _mixer_call(functools.partial(_rwkv_kernel, NSUB=nsub, G=1, C=c, has_state=False), "rwkv_prompt", p, t,
                         bp, lp // rows_p, rows_p, 0, cols, [], params, None, sshape(bp, 1), scratch(1), layer=l)
    nseq = nsub_s * g
    rows_s = nseq * ls
    y, s_s = _mixer_call(functools.partial(_rwkv_kernel, NSUB=nsub_s, G=g, C=ls, has_state=True), "rwkv_sample", p, t,
                         bs // nseq, 1, rows_s, tp // rows_s, cols,
                         [(sh_rows_s, (rows_s, RWKV_COLS)), (s0_s, (nseq, 8, 64, 64))], params, y, sshape(bs, nseq),
                         scratch(nseq), layer=l)
    return y, s_p, s_s


def _rope_tables(pos):
    half = ROT_DIM // 2
    inv = ROPE_THETA ** (-(jnp.arange(half, dtype=F32) * 2.0) / ROT_DIM)
    ang = pos.astype(F32)[:, None] * inv
    cos, sin = jnp.cos(ang), jnp.sin(ang)
    n = pos.shape[0]
    zeros = jnp.zeros((n, half), F32)
    rest = SWA_HEADDIM - ROT_DIM
    cos64 = jnp.concatenate([cos, cos, jnp.ones((n, rest), F32)], axis=1)
    sa64 = jnp.concatenate([-sin, zeros, jnp.zeros((n, rest), F32)], axis=1)
    sb64 = jnp.concatenate([zeros, sin, jnp.zeros((n, rest), F32)], axis=1)
    return tuple(jnp.tile(t, (1, GW // SWA_HEADDIM)) for t in (cos64, sa64, sb64))


def _pad_rows_front(x, axis, total):
    pad = [(0, 0)] * x.ndim
    pad[axis] = (total - x.shape[axis], 0)
    return jnp.pad(x, pad)


def kernel(x_prompt, x_sample, state_ssm, state_ssm_conv, state_rwkv, state_rwkv_shift, cache_swa_k,
           cache_swa_v, state_lru, state_lru_conv, norm_mix, w_in, ssm_conv_w, ssm_conv_b, ssm_dt_bias,
           ssm_a_log, ssm_d, ssm_norm, rwkv_mu, rwkv_w0, rwkv_w2, rwkv_a0, rwkv_a2, rwkv_g2, rwkv_k_k,
           rwkv_k_a, rwkv_r_k, rwkv_ln_w, rwkv_ln_b, swa_sinks, lru_conv_w, lru_conv_b, lru_wa, lru_ba,
           lru_wi, lru_bi, lru_lambda, w_out, norm_mlp, mlp_w1, mlp_w2, norm_final):
    bp, lp, _ = x_prompt.shape
    bs, ls, _ = x_sample.shape
    tp, ts = bp * lp, bs * ls
    t = tp + ts
    dims = (t, bp, lp, bs, ls)
    depth = w_in.shape[0]

    w_in_r = _winprep(w_in.astype(BF16))
    w_out_b = w_out.astype(BF16)
    w1_b = mlp_w1[0].astype(BF16)
    w2_b = mlp_w2[0].astype(BF16)
    eye8 = jnp.eye(8, dtype=F32)
    lru_wa_d = jnp.einsum('lncd,nm->lncmd', lru_wa, eye8).reshape(depth, GW, GW).astype(BF16)
    lru_wi_d = jnp.einsum('lncd,nm->lncmd', lru_wi, eye8).reshape(depth, GW, GW).astype(BF16)
    z64 = jnp.zeros((depth, 64, GW), F32)
    rwkv_w2_p = jnp.concatenate([rwkv_w2, z64], axis=1).astype(BF16)
    rwkv_a2_p = jnp.concatenate([z64, rwkv_a2], axis=1).astype(BF16)
    rwkv_g2_b = rwkv_g2.astype(BF16)
    dtb_p = jnp.pad(ssm_dt_bias, ((0, 0), (0, 128 - SSM_HEADS)))
    alog_p = jnp.pad(ssm_a_log, ((0, 0), (0, 128 - SSM_HEADS)))
    d_e = jnp.repeat(ssm_d, 64, axis=-1)
    sinks_b = jnp.broadcast_to(swa_sinks[:, :, None], (depth, 8, 128))
    rope_p = _rope_tables(jnp.arange(lp))
    rope_s = _rope_tables(PAST_LEN + jnp.arange(ls))

    def r2(a, l):
        return a[l][None, :]

    def rows(a):
        return a[:, None, :]

    ssm_params = [ssm_conv_w, rows(ssm_conv_b), rows(dtb_p), rows(alog_p), rows(d_e), rows(ssm_norm)]
    rwkv_params = [rows(rwkv_mu), rows(rwkv_w0), rwkv_w2_p, rows(rwkv_a0), rwkv_a2_p, rwkv_g2_b, rows(rwkv_k_k),
                   rows(rwkv_k_a), rwkv_r_k.reshape(depth, 1, GW), rows(rwkv_ln_w), rows(rwkv_ln_b)]
    lru_params = [lru_conv_w, rows(lru_conv_b), lru_wa_d, rows(lru_ba), lru_wi_d, rows(lru_bi), rows(lru_lambda)]

    ssm_buf = _pad_rows_front(state_ssm_conv, 2, HIST)
    lru_buf = _pad_rows_front(state_lru_conv, 2, HIST)
    lru_h0 = state_lru[:, :, None, :]
    sh_rows = jnp.pad(state_rwkv_shift[:, :, None, :], ((0, 0), (0, 0), (0, ls - 1), (0, 0))).reshape(depth, ts, RWKV_COLS)
    swa_k0 = cache_swa_k.reshape(depth, bs, WINDOW, 128)
    swa_v0 = cache_swa_v.reshape(depth, bs, WINDOW, 128)

    x = jnp.concatenate([x_prompt.reshape(tp, D_MODEL), x_sample.reshape(ts, D_MODEL)], axis=0)

    names = ('ssm_p', 'ssm_s', 'conv_p', 'conv_s', 'rwkv_p', 'rwkv_s', 'shift_p', 'shift_s',
             'k_p', 'k_s', 'v_p', 'v_s', 'lru_p', 'lru_s', 'lconv_p', 'lconv_s')
    outs = {n: [] for n in names}
    for l in range(depth):
        p = _inproj(x, r2(norm_mix, l), w_in_r, l)

        y_ssm, hs_p, hs_s = _run_ssm(p, dims, l, ssm_params, ssm_buf, state_ssm)
        y_rwkv, s_p, s_s = _run_rwkv(p, dims, l, rwkv_params, sh_rows, state_rwkv)
        y_swa, k_p, v_p, k_s, v_s = _run_swa(p, dims, l, rope_p, rope_s, sinks_b, swa_k0, swa_v0)
        y_lru, lh_p, lh_s = _run_lru(p, dims, l, lru_params, lru_buf, lru_h0)

        x1 = _outproj([y_ssm, y_rwkv, y_swa, y_lru], w_out_b, l, x)
        if l + 1 < depth:
            x, w1_b, w2_b = _mlp(x1, r2(norm_mlp, l), w1_b, w2_b, norm_final[None, :], False, mlp_w1, mlp_w2, l + 1)
        else:
            (x,) = _mlp(x1, r2(norm_mlp, l), w1_b, w2_b, norm_final[None, :], True, None, None, None)

        def tail_p(n, c0, c1):
            return jnp.stack([p[b * lp + lp - n:b * lp + lp, c0:c1] for b in range(bp)], axis=0)

        ps = p[tp:].reshape(bs, ls, PW)
        lora0 = CB_LORA * 256
        outs['ssm_p'].append(hs_p)
        outs['ssm_s'].append(hs_s)
        outs['conv_p'].append(tail_p(3, 0, SSM_XBC))
        outs['conv_s'].append(ps[:, ls - 3:, 0:SSM_XBC])
        outs['rwkv_p'].append(s_p)
        outs['rwkv_s'].append(s_s)
        outs['shift_p'].append(jnp.concatenate([tail_p(1, 5 * GW, 8 * GW), tail_p(1, lora0, lora0 + 256)], axis=-1)[:, 0])
        outs['shift_s'].append(jnp.concatenate([ps[:, ls - 1, 5 * GW:8 * GW], ps[:, ls - 1, lora0:lora0 + 256]], axis=-1))
        outs['k_p'].append(k_p.reshape(bp, WINDOW, 2, SWA_HEADDIM))
        outs['k_s'].append(k_s.reshape(bs, WINDOW, 2, SWA_HEADDIM))
        outs['v_p'].append(v_p.reshape(bp, WINDOW, 2, SWA_HEADDIM))
        outs['v_s'].append(v_s.reshape(bs, WINDOW, 2, SWA_HEADDIM))
        outs['lru_p'].append(lh_p)
        outs['lru_s'].append(lh_s)
        outs['lconv_p'].append(tail_p(3, SSM_XBC, SSM_XBC + GW))
        outs['lconv_s'].append(ps[:, ls - 3:, SSM_XBC:SSM_XBC + GW])

    st = {n: jnp.stack(v, axis=0) for n, v in outs.items()}
    return (x[:tp].reshape(bp, lp, D_MODEL), x[tp:].reshape(bs, ls, D_MODEL),
            st['ssm_p'], st['ssm_s'], st['conv_p'], st['conv_s'], st['rwkv_p'], st['rwkv_s'],
            st['shift_p'], st['shift_s'], st['k_p'], st['k_s'], st['v_p'], st['v_s'],
            st['lru_p'], st['lru_s'], st['lconv_p'], st['lconv_s'])
```

```python
import functools
import math

import jax
import jax.numpy as jnp
from jax import lax
from jax.experimental import pallas as pl
from jax.experimental.pallas import tpu as pltpu

F32 = jnp.float32
BF16 = jnp.bfloat16

D_MODEL = 2048
GW = 512
RMS_EPS = 1e-6
CONV_W = 4
SSM_HEADS = 8
SSM_STATE = 128
SSM_XBC = 1024
RWKV_COLS = 1792
RWKV_GN_EPS = 64e-5
SWA_HEADDIM = 64
WINDOW = 128
ROT_DIM = 16
ROPE_THETA = 500000.0
LRU_C = 8.0
D_FF = 4 * D_MODEL
PAST_LEN = 16384

PW = 5376
CB_XBC, CB_LRU = 0, 1
CB_Z, CB_R, CB_K, CB_V, CB_Q = 4, 5, 6, 7, 8
CB_KV, CB_LORA = 18, 19
CB_DT = 40

HIST = 8
VMEM_LIMIT = 56 * 1024 * 1024

TILES = dict(
    prep_tk=256, in_tm=1408, in_tn=768, out_tm=704, mlp_tm=704, mlp_tf=1024, mlp_slabs=64,
    ssm_q=128, ssm_nb=4, swa_q=128, swa_nb=4, lru_r=256, rwkv_c=64, rwkv_sub=8,
    ssm_ns=4, swa_ns=8, rwkv_g=4, rwkv_sub_s=2,
)


def _cparams(sem):
    return pltpu.CompilerParams(dimension_semantics=sem, vmem_limit_bytes=VMEM_LIMIT)


def _mm(a, b):
    return jnp.dot(a.astype(BF16), b.astype(BF16), preferred_element_type=F32)


def _mm_nt(a, b):
    return lax.dot_general(a.astype(BF16), b.astype(BF16), (((1,), (1,)), ((), ())),
                           preferred_element_type=F32)


def _mm_tn(a, b):
    return lax.dot_general(a.astype(BF16), b.astype(BF16), (((0,), (0,)), ((), ())),
                           preferred_element_type=F32)


def _split3(x):
    x1 = x.astype(BF16)
    r = x - x1.astype(F32)
    x2 = r.astype(BF16)
    r = r - x2.astype(F32)
    return x1, x2, r.astype(BF16)


def _mm_exact_rhs(a, sel):
    p1, p2, p3 = _split3(a)
    return _mm(p1, sel) + _mm(p2, sel) + _mm(p3, sel)


def _mm_split2_rhs(a, sel):
    p1 = a.astype(BF16)
    p2 = (a - p1.astype(F32)).astype(BF16)
    return _mm(p1, sel) + _mm(p2, sel)


def _mm_exact_lhs(sel, b):
    p1, p2, p3 = _split3(b)
    return _mm(sel, p1) + _mm(sel, p2) + _mm(sel, p3)


def _mm_nt_exact_lhs(sel, b):
    p1, p2, p3 = _split3(b)
    return _mm_nt(sel, p1) + _mm_nt(sel, p2) + _mm_nt(sel, p3)


def _softplus(x):
    return jnp.maximum(x, 0.0) + jnp.log1p(jnp.exp(-jnp.abs(x)))


def _sigmoid(x):
    return 1.0 / (1.0 + jnp.exp(-x))


def _iota(shape, dim):
    return lax.broadcasted_iota(jnp.int32, shape, dim)


def _ind(mask):
    return jnp.where(mask, 1.0, 0.0)


def _rms(x, g):
    ms = jnp.mean(x * x, axis=-1, keepdims=True)
    return x * lax.rsqrt(ms + RMS_EPS) * g


def _conv_from_hist(ext_ref, rows, cw_ref, cb_ref):
    acc = cb_ref[...] + cw_ref[0:1, :] * ext_ref[HIST - 3:HIST - 3 + rows, :]
    for j in range(1, CONV_W):
        acc = acc + cw_ref[j:j + 1, :] * ext_ref[HIST - 3 + j:HIST - 3 + j + rows, :]
    return acc


def _winprep_kernel(w_ref, o_ref):
    o_rwkv = GW + SSM_XBC + SSM_HEADS
    o_swa = o_rwkv + RWKV_COLS
    o_lru = o_swa + GW + 256
    sections = ((GW, SSM_XBC), (o_lru, 2 * GW), (0, GW), (o_rwkv, 3 * GW), (o_swa, GW), (o_swa + GW, 256),
                (o_rwkv + 3 * GW, 256), (GW + SSM_XBC, SSM_HEADS))
    dst = 0
    for src, width in sections:
        o_ref[:, dst:dst + width] = w_ref[:, src:src + width]
        dst += width
    o_ref[:, dst:PW] = jnp.zeros((o_ref.shape[0], PW - dst), BF16)


def _inproj_kernel(x_ref, g_ref, w_ref, o_ref, h_ref):
    @pl.when(pl.program_id(1) == 0)
    def _():
        h_ref[...] = _rms(x_ref[...], g_ref[...]).astype(BF16)

    o_ref[...] = jnp.dot(h_ref[...], w_ref[...], preferred_element_type=F32)


def _outproj_kernel(y0_ref, y1_ref, y2_ref, y3_ref, w_ref, x_ref, x1_ref):
    acc = x_ref[...]
    for i, y_ref in enumerate((y0_ref, y1_ref, y2_ref, y3_ref)):
        acc = acc + jnp.dot(y_ref[...], w_ref[i * GW:(i + 1) * GW, :], preferred_element_type=F32)
    x1_ref[...] = acc


def _mlp_kernel(*refs, final, n_cast):
    x_ref, g_ref, w1_ref, w2_ref, gf_ref = refs[:5]
    xo_ref, h_ref = refs[5 + n_cast], refs[-1]
    for src_ref, dst_ref in zip(refs[5:5 + n_cast], refs[6 + n_cast:6 + 2 * n_cast]):
        dst_ref[...] = src_ref[...].astype(BF16)
    k = pl.program_id(1)

    @pl.when(k == 0)
    def _():
        x = x_ref[...]
        h_ref[...] = _rms(x, g_ref[...]).astype(BF16)
        xo_ref[...] = x

    a = jnp.dot(h_ref[...], w1_ref[...], preferred_element_type=F32)
    a = jnp.square(jnp.maximum(a, 0.0)).astype(BF16)
    xo_ref[...] += jnp.dot(a, w2_ref[...], preferred_element_type=F32)

    if final:
        @pl.when(k == pl.num_programs(1) - 1)
        def _():
            xo_ref[...] = _rms(xo_ref[...], gf_ref[...])


def _ssm_kernel(*refs, Q, NS, has_state):
    if has_state:
        (xbc_ref, z_ref, dt_ref, buf_ref, h0_ref, cw_ref, cb_ref, dtb_ref, alog_ref, de_ref, nw_ref,
         _, y_ref, hout_ref, ext_ref, h_ref) = refs
    else:
        (xbc_ref, z_ref, dt_ref, cw_ref, cb_ref, dtb_ref, alog_ref, de_ref, nw_ref,
         y_ref, hout_ref, ext_ref, h_ref) = refs
    c = pl.program_id(1)
    n_slots = NS if has_state else 1

    @pl.when(c == 0)
    def _():
        if has_state:
            for slot in range(n_slots):
                ext_ref[slot, 0:HIST, :] = buf_ref[slot]
                for g in range(2):
                    h_ref[slot, g] = h0_ref[slot, 4 * g:4 * g + 4].reshape(256, SSM_STATE).T
        else:
            ext_ref[0, 0:HIST, :] = jnp.zeros((HIST, SSM_XBC), F32)
            h_ref[...] = jnp.zeros(h_ref.shape, F32)

    a_neg = jnp.where(_iota((1, 128), 1) < SSM_HEADS, -jnp.exp(alog_ref[...]), 0.0)
    tri = (_iota((Q, Q), 0) >= _iota((Q, Q), 1))
    tri16 = _ind(tri).astype(BF16)
    expand = _ind(_iota((128, GW), 0) == (_iota((128, GW), 1) >> 6)).astype(BF16)
    sel = _ind(_iota((SSM_HEADS, 128), 0) == _iota((SSM_HEADS, 128), 1)).astype(BF16)
    lanehead = _iota((1, 256), 1) >> 6

    for s in range(NS):
        slot = s if has_state else 0
        rows = slice(s * Q, (s + 1) * Q)
        ext = ext_ref.at[slot]
        ext[HIST:HIST + Q, :] = xbc_ref[rows, :]
        pre = _conv_from_hist(ext, Q, cw_ref, cb_ref)
        ext[0:HIST, :] = ext[Q:Q + HIST, :]
        xbc = pre * _sigmoid(pre)
        x = xbc[:, 0:GW]
        bm = xbc[:, GW:GW + 256]
        cm = xbc[:, GW + 256:GW + 512]

        dt = _softplus(dt_ref[rows, :] + dtb_ref[...])
        cum = _mm_exact_lhs(tri16, dt * a_neg)
        dt_e = _mm_exact_rhs(dt, expand)
        cum_e = _mm_exact_rhs(cum, expand)
        cum_t = _mm_nt_exact_lhs(sel, cum)
        last = cum_e[Q - 1:Q, :]
        expc_e = jnp.exp(cum_e)
        tail_e = jnp.exp(last - cum_e)
        dtot_e = jnp.exp(last)
        dx = dt_e * x

        ys = []
        for g in range(2):
            bg = bm[:, g * 128:(g + 1) * 128]
            cg = cm[:, g * 128:(g + 1) * 128]
            cb = _mm_nt(cg, bg)
            h_t = h_ref[slot, g]
            yg = _mm(cg, h_t) * expc_e[:, g * 256:(g + 1) * 256]
            dxg = dx[:, g * 256:(g + 1) * 256]
            for hh in range(4):
                h = 4 * g + hh
                decay = jnp.exp(jnp.minimum(cum[:, h:h + 1] - cum_t[h:h + 1, :], 0.0))
                m = jnp.where(tri, cb * decay, 0.0)
                yg = yg + _mm(m, jnp.where(lanehead == hh, dxg, 0.0))
            ys.append(yg)
            h_ref[slot, g] = (h_t * dtot_e[:, g * 256:(g + 1) * 256]
                              + _mm_tn(bg, dxg * tail_e[:, g * 256:(g + 1) * 256]))

        y = jnp.concatenate(ys, axis=1) + de_ref[...] * x
        z = z_ref[rows, :]
        y = y * (z * _sigmoid(z))
        outs = []
        for g in range(2):
            yg = y[:, g * 256:(g + 1) * 256]
            outs.append(yg * lax.rsqrt(jnp.mean(yg * yg, axis=-1, keepdims=True) + RMS_EPS))
        y_ref[rows, :] = (jnp.concatenate(outs, axis=1) * nw_ref[...]).astype(y_ref.dtype)

    @pl.when(c == pl.num_programs(1) - 1)
    def _():
        for slot in range(n_slots):
            for g in range(2):
                hout_ref[slot, 4 * g:4 * g + 4] = h_ref[slot, g].T.reshape(4, 64, SSM_STATE)


def _lru_kernel(*refs, R, has_state):
    if has_state:
        (u_ref, buf_ref, h0_ref, cw_ref, cb_ref, wa_ref, ba_ref, wi_ref, bi_ref, lam_ref,
         _, y_ref, hout_ref, ext_ref, h_ref) = refs
    else:
        (u_ref, cw_ref, cb_ref, wa_ref, ba_ref, wi_ref, bi_ref, lam_ref,
         y_ref, hout_ref, ext_ref, h_ref) = refs
    c = pl.program_id(1)

    @pl.when(c == 0)
    def _():
        if has_state:
            ext_ref[0:HIST, :] = buf_ref[0]
            h_ref[...] = h0_ref[0]
        else:
            ext_ref[0:HIST, :] = jnp.zeros((HIST, GW), F32)
            h_ref[...] = jnp.zeros(h_ref.shape, F32)

    u = u_ref[...]
    gate = u[:, GW:]
    ext_ref[HIST:HIST + R, :] = u[:, :GW]
    xc = _conv_from_hist(ext_ref, R, cw_ref, cb_ref)
    ext_ref[0:HIST, :] = ext_ref[R:R + HIST, :]

    a, b, gelu = _lru_gates(xc, gate, wa_ref, ba_ref, wi_ref, bi_ref, lam_ref)

    row = _iota((R, 1), 0)
    s = 1
    while s < R:
        keep = row >= s
        a_prev = jnp.where(keep, pltpu.roll(a, s, 0), 1.0)
        b_prev = jnp.where(keep, pltpu.roll(b, s, 0), 0.0)
        b = b + a * b_prev
        a = a * a_prev
        s *= 2
    h = b + a * h_ref[...]
    h_ref[...] = h[R - 1:R, :]
    hout_ref[0] = h[R - 1:R, :]

    y_ref[...] = (h * gelu).astype(y_ref.dtype)


def _lru_gates(xc, gate, wa_ref, ba_ref, wi_ref, bi_ref, lam_ref):
    rg = _sigmoid(_mm(xc, wa_ref[...]) + ba_ref[...])
    ig = _sigmoid(_mm(xc, wi_ref[...]) + bi_ref[...])
    log_a = -LRU_C * rg * _softplus(-lam_ref[...])
    th = jnp.tanh(log_a)
    b = jnp.sqrt(-2.0 * th / (1.0 - th)) * (ig * xc)
    gelu = 0.5 * gate * (1.0 + jnp.tanh(math.sqrt(2.0 / math.pi) * (gate + 0.044715 * (gate * gate * gate))))
    return jnp.exp(log_a), b, gelu


def _lru_short_kernel(u_ref, buf_ref, h0_ref, cw_ref, cb_ref, wa_ref, ba_ref, wi_ref, bi_ref, lam_ref, _,
                      y_ref, hall_ref, ext_ref, *, N):
    rows = N * HIST
    u = u_ref[...]
    xb = u[:, :GW].reshape(N, HIST, GW)
    ext_ref[0:HIST, :] = jnp.zeros((HIST, GW), F32)
    ext_ref[HIST:HIST + 2 * rows, :] = jnp.concatenate([buf_ref[0], xb], axis=1).reshape(2 * rows, GW)
    conv = _conv_from_hist(ext_ref, 2 * rows, cw_ref, cb_ref)
    xc = conv.reshape(N, 2 * HIST, GW)[:, HIST:, :].reshape(rows, GW)
    a, b, gelu = _lru_gates(xc, u[:, GW:], wa_ref, ba_ref, wi_ref, bi_ref, lam_ref)
    a, b = a.reshape(N, HIST, GW), b.reshape(N, HIST, GW)
    pos = _iota((1, HIST, 1), 1)
    s = 1
    while s < HIST:
        keep = pos >= s
        a_prev = jnp.where(keep, pltpu.roll(a, s, 1), 1.0)
        b_prev = jnp.where(keep, pltpu.roll(b, s, 1), 0.0)
        b = b + a * b_prev
        a = a * a_prev
        s *= 2
    h = (b + a * h0_ref[0]).reshape(rows, GW)
    hall_ref[0] = h
    y_ref[...] = (h * gelu).astype(y_ref.dtype)


def _swa_kernel(*refs, QB, NS, has_state):
    if has_state:
        (q_ref, kv_ref, kc_ref, vc_ref, cos_ref, sa_ref, sb_ref, sink_ref,
         _, y_ref, kout_ref, vout_ref, hk_ref, hv_ref) = refs
    else:
        (q_ref, kv_ref, cos_ref, sa_ref, sb_ref, sink_ref,
         y_ref, kout_ref, vout_ref, hk_ref, hv_ref) = refs
    c = pl.program_id(1)
    nk = WINDOW + QB
    lq = QB.bit_length() - 1

    @pl.when(c == 0)
    def _():
        if has_state:
            hk_ref[...] = kc_ref[...]
            hv_ref[...] = vc_ref[...]
        else:
            hk_ref[...] = jnp.zeros(hk_ref.shape, F32)
            hv_ref[...] = jnp.zeros(hv_ref.shape, F32)

    def rope(x, width, trows):
        cos, sa, sb = cos_ref[trows, 0:width], sa_ref[trows, 0:width], sb_ref[trows, 0:width]
        return x * cos + pltpu.roll(x, width - ROT_DIM // 2, 1) * sa + pltpu.roll(x, ROT_DIM // 2, 1) * sb

    kj = _iota((nk, 4 * QB), 0)
    qi = _iota((nk, 4 * QB), 1) & (QB - 1)
    neg_mask = jnp.where(kj >= qi, jnp.where(kj <= qi + WINDOW, 0.0, -jnp.inf), -jnp.inf)
    neg_mask0 = neg_mask if has_state else jnp.where(c == 0, jnp.where(kj >= WINDOW, neg_mask, -jnp.inf), neg_mask)
    upper = _iota((1, 128), 1) >= SWA_HEADDIM
    lower = _iota((1, 128), 1) < SWA_HEADDIM
    chead = _iota((1, 4 * QB), 1) >> lq
    sink_rows = []
    for g in range(2):
        sr = sink_ref[4 * g + 3:4 * g + 4, 0:1]
        for hh in (2, 1, 0):
            sr = jnp.where(chead == hh, sink_ref[4 * g + hh:4 * g + hh + 1, 0:1], sr)
        sink_rows.append(sr)

    assert has_state or QB == WINDOW
    qs, keys, vals = [], [], []
    for s in range(NS):
        rows = slice(s * QB, (s + 1) * QB)
        trows = slice(0, QB) if has_state else rows
        qs.append(rope(q_ref[rows, :], GW, trows) * (SWA_HEADDIM ** -0.5))
        kv = kv_ref[rows, :]
        k_new = rope(kv[:, 0:128], 128, trows)
        v_new = kv[:, 128:256]
        if has_state or s == 0:
            hist_k, hist_v = hk_ref[s], hv_ref[s]
        else:
            hist_k, hist_v = keys[s - 1][WINDOW:, :], vals[s - 1][WINDOW:, :]
        keys.append(jnp.concatenate([hist_k, k_new], axis=0))
        vals.append(jnp.concatenate([hist_v, v_new], axis=0))

    pairs = [(s, g) for s in range(NS) for g in range(2)]
    vds, scores = [], []
    for s, g in pairs:
        own = upper if g == 1 else lower
        kd = jnp.where(own, keys[s], pltpu.roll(keys[s], SWA_HEADDIM, 1))
        vds.append(jnp.where(own, vals[s], pltpu.roll(vals[s], SWA_HEADDIM, 1)))
        parts = []
        for hh in range(4):
            h = 4 * g + hh
            mine = upper if h % 2 == 1 else lower
            parts.append(jnp.where(mine, qs[s][:, 128 * (h // 2):128 * (h // 2 + 1)], 0.0))
        mask = neg_mask0 if s == 0 else neg_mask
        scores.append(_mm_nt(kd, jnp.concatenate(parts, axis=0)) + mask)
    probs = []
    for (s, g), sc in zip(pairs, scores):
        m = jnp.maximum(jnp.max(sc, axis=0, keepdims=True), sink_rows[g])
        e = jnp.exp(sc - m)
        den = jnp.sum(e, axis=0, keepdims=True) + jnp.exp(sink_rows[g] - m)
        probs.append(e / den)
    outs = [_mm_tn(pr, vd) for pr, vd in zip(probs, vds)]

    for s in range(NS):
        tiles = [None] * 4
        for g in range(2):
            o = outs[2 * s + g]
            for hh in range(4):
                h = 4 * g + hh
                oh = o[hh * QB:(hh + 1) * QB]
                tiles[h // 2] = oh if h % 2 == 0 else jnp.where(upper, oh, tiles[h // 2])
        y_ref[s * QB:(s + 1) * QB, :] = jnp.concatenate(tiles, axis=1).astype(y_ref.dtype)
        if has_state:
            hk_ref[s] = keys[s][QB:, :]
            hv_ref[s] = vals[s][QB:, :]
    if not has_state:
        hk_ref[0] = keys[NS - 1][WINDOW:, :]
        hv_ref[0] = vals[NS - 1][WINDOW:, :]
    kout_ref[...] = hk_ref[...]
    vout_ref[...] = hv_ref[...]


def _seg_sum(x):
    ones = _ind((_iota((256, 256), 0) >> 6) == (_iota((256, 256), 1) >> 6)).astype(BF16)
    return jnp.concatenate([_mm_split2_rhs(x[:, 0:256], ones), _mm_split2_rhs(x[:, 256:512], ones)], axis=1)


def _rwkv_kernel(*refs, NSUB, G, C, has_state):
    if has_state:
        (r_ref, k_ref, v_ref, lo_ref, sh_ref, h0_ref, mu_ref, w0_ref, w2_ref, a0_ref, a2_ref, g2_ref,
         kk_ref, ka_ref, rk_ref, lnw_ref, lnb_ref, _, y_ref, hout_ref, hs_ref, carry_ref) = refs
    else:
        (r_ref, k_ref, v_ref, lo_ref, mu_ref, w0_ref, w2_ref, a0_ref, a2_ref, g2_ref,
         kk_ref, ka_ref, rk_ref, lnw_ref, lnb_ref, y_ref, hout_ref, hs_ref, carry_ref) = refs
    c = pl.program_id(1)
    R = G * C
    RB = NSUB * R
    lc = C.bit_length() - 1
    n_slots = NSUB * G if has_state else 1

    @pl.when(c == 0)
    def _():
        if has_state:
            for slot in range(n_slots):
                for hh in range(2):
                    rows = [jnp.pad(h0_ref[slot, 4 * hh + h], ((0, 0), (64 * h, 192 - 64 * h))) for h in range(4)]
                    hs_ref[slot, hh] = jnp.concatenate(rows, axis=0)
        else:
            hs_ref[...] = jnp.zeros(hs_ref.shape, F32)
            carry_ref[...] = jnp.zeros(carry_ref.shape, F32)

    row = _iota((RB, 1), 0)
    first = ((row & (C - 1)) == 0) if has_state else (row == 0)
    mixed = []
    for ref, off, width in ((r_ref, 0, GW), (k_ref, GW, GW), (v_ref, 2 * GW, GW), (lo_ref, 3 * GW, 256)):
        pf = ref[...]
        if has_state:
            fv = sh_ref[:, off:off + width]
        else:
            fv = carry_ref[0:1, off:off + width]
        shifted = jnp.where(first, fv, pltpu.roll(pf, 1, 0))
        mixed.append(pf + (shifted - pf) * mu_ref[:, off:off + width])
        if not has_state:
            carry_ref[0:1, off:off + width] = pf[RB - 1:RB, :]
    r, k, v, lo = mixed

    lo_wa = lo[:, 0:128]
    w_log = -_softplus(-(w0_ref[...] + _mm(jnp.tanh(lo_wa), w2_ref[...]))) - 0.5
    lw = -jnp.exp(w_log)
    a = _sigmoid(a0_ref[...] + _mm(lo_wa, a2_ref[...]))
    gate = _mm(_sigmoid(lo[:, 128:256]), g2_ref[...])
    kk = k * kk_ref[...]
    kmod = k * (1.0 + (a - 1.0) * ka_ref[...])
    kk = kk * lax.rsqrt(jnp.maximum(_seg_sum(kk * kk), 1e-24))
    kb = kk * a

    NB = 4 * R
    lanehead = _iota((1, 256), 1) >> 6
    bi, bj = _iota((NB, NB), 0), _iota((NB, NB), 1)
    same = _ind((bi >> lc) == (bj >> lc))
    strict = same * _ind(bj < bi)
    incl = same * _ind(bj <= bi)
    eye_nb = _ind(bi == bj)
    si, sj = _iota((R, R), 0), _iota((R, R), 1)
    seg = _ind((si >> lc) == (sj >> lc))
    seg_ones = seg.astype(BF16)
    seg_tri = (seg * _ind(sj <= si)).astype(BF16)

    def bd(x):
        parts = []
        for s in range(G):
            xs = x[s * C:(s + 1) * C]
            for h in range(4):
                parts.append(jnp.where(lanehead == h, xs, 0.0))
        return jnp.concatenate(parts, axis=0)

    chains = [(sub, hh) for sub in range(NSUB) for hh in range(2)]
    ops = []
    for sub, hh in chains:
        sr, sl = slice(sub * R, (sub + 1) * R), slice(hh * 256, (hh + 1) * 256)
        lw4 = lw[sr, sl]
        clw = _mm_exact_lhs(seg_tri, lw4)
        clt = clw[C - 1:C, :] if G == 1 else _mm_exact_lhs(seg_ones, lw4)
        e_out = jnp.exp(-clw)
        e_end = jnp.exp(clt - clw)
        b_rq = bd(r[sr, sl] * jnp.exp(clw))
        ops.append(dict(
            kq=bd(kk[sr, sl] * jnp.exp(clw - lw4)).astype(BF16), rq=b_rq, rq16=b_rq.astype(BF16),
            kk=bd(kmod[sr, sl] * e_out).astype(BF16), bk=bd(kb[sr, sl] * e_out).astype(BF16),
            kk2=bd(kmod[sr, sl] * e_end).astype(BF16), bk2=bd(kb[sr, sl] * e_end).astype(BF16),
            v=bd(v[sr, sl]).astype(BF16), p_end=jnp.exp(clt)))

    n_mat = [strict * _mm_nt(o['kq'], o['bk']) for o in ops]
    a_kk = [strict * _mm_nt(o['kq'], o['kk']) for o in ops]
    a_rb = [incl * _mm_nt(o['rq16'], o['bk']) for o in ops]
    a_rk = [incl * _mm_nt(o['rq16'], o['kk']) for o in ops]

    t_inv = [eye_nb - n for n in n_mat]
    pw = n_mat
    for _ in range(lc - 1):
        pw = [_mm(x, x) for x in pw]
        t_inv = [t + _mm(t, x) for t, x in zip(t_inv, pw)]

    av = [_mm(a, o['v']) for a, o in zip(a_kk, ops)]
    wu = [_mm(t, jnp.concatenate([o['kq'], x.astype(BF16)], axis=1)) for t, o, x in zip(t_inv, ops, av)]
    arb_wu = [_mm(a, x) for a, x in zip(a_rb, wu)]
    rq2 = [o['rq'] - x[:, 0:256] for o, x in zip(ops, arb_wu)]
    y_loc = [_mm(a, o['v']) - x[:, 256:512] for a, o, x in zip(a_rk, ops, arb_wu)]
    wu16 = [x.astype(BF16) for x in wu]

    upd = []
    for i, o in enumerate(ops):
        per_seg = []
        for s in range(G):
            rs = slice(s * 4 * C, (s + 1) * 4 * C)
            bwq = _mm_tn(o['bk2'][rs], wu16[i][rs, 0:256])
            add = _mm_tn(o['v'][rs], o['kk2'][rs]) - _mm_tn(wu16[i][rs, 256:512], o['bk2'][rs])
            per_seg.append((bwq.astype(BF16), add))
        upd.append(per_seg)

    y_parts = {}
    for i, (sub, hh) in enumerate(chains):
        ys = []
        for s in range(G):
            rs = slice(s * 4 * C, (s + 1) * 4 * C)
            slot = sub * G + s if has_state else 0
            g0 = hs_ref[slot, hh]
            g0b = g0.astype(BF16)
            yb = _mm_nt(rq2[i][rs], g0b) + y_loc[i][rs]
            y4 = yb[0:C]
            for h in range(1, 4):
                y4 = y4 + yb[h * C:(h + 1) * C]
            ys.append(y4)
            bwq, add = upd[i][s]
            hs_ref[slot, hh] = g0 * ops[i]['p_end'][s * C:s * C + 1, :] - _mm_nt(g0b, bwq) + add
        y_parts[(sub, hh)] = ys[0] if G == 1 else jnp.concatenate(ys, axis=0)
    y_subs = [jnp.concatenate([y_parts[(sub, 0)], y_parts[(sub, 1)]], axis=1) for sub in range(NSUB)]
    y = y_subs[0] if NSUB == 1 else jnp.concatenate(y_subs, axis=0)

    mean = _seg_sum(y) * (1.0 / 64.0)
    yc = y - mean
    var = _seg_sum(yc * yc) * (1.0 / 64.0)
    yn = yc * lax.rsqrt(var + RWKV_GN_EPS) * lnw_ref[...] + lnb_ref[...]
    bonus = _seg_sum(r * kmod * rk_ref[...]) * v
    y_ref[...] = ((yn + bonus) * gate).astype(y_ref.dtype)

    @pl.when(c == pl.num_programs(1) - 1)
    def _():
        for slot in range(n_slots):
            for hh in range(2):
                for h in range(4):
                    hout_ref[slot, 4 * hh + h] = hs_ref[slot, hh, 64 * h:64 * h + 64, 64 * h:64 * h + 64]


def _row(n):
    return pl.BlockSpec((1, n), lambda *_: (0, 0))


def _full(shape):
    nd = len(shape)
    return pl.BlockSpec(shape, lambda *_: (0,) * nd)


def _any():
    return pl.BlockSpec(memory_space=pl.ANY)


def _winprep(w_in):
    depth, d, cols = w_in.shape
    tk = TILES['prep_tk']
    return pl.pallas_call(
        _winprep_kernel,
        grid=(depth, d // tk),
        in_specs=[pl.BlockSpec((None, tk, cols), lambda l, i: (l, i, 0))],
        out_specs=pl.BlockSpec((None, tk, PW), lambda l, i: (l, i, 0)),
        out_shape=jax.ShapeDtypeStruct((depth, d, PW), BF16),
        compiler_params=_cparams(("parallel", "parallel")),
        name="winprep",
    )(w_in)


def _inproj(x, g, w_all, l):
    t, tm, tn = x.shape[0], TILES['in_tm'], TILES['in_tn']
    return pl.pallas_call(
        _inproj_kernel,
        grid=(t // tm, PW // tn),
        in_specs=[pl.BlockSpec((tm, D_MODEL), lambda i, j: (i, 0)), _row(D_MODEL),
                  pl.BlockSpec((None, D_MODEL, tn), lambda i, j: (l, 0, j))],
        out_specs=pl.BlockSpec((tm, tn), lambda i, j: (i, j)),
        out_shape=jax.ShapeDtypeStruct((t, PW), F32),
        scratch_shapes=[pltpu.VMEM((tm, D_MODEL), BF16)],
        compiler_params=_cparams(("parallel", "arbitrary")),
        name="inproj",
    )(x, g, w_all)


def _outproj(ys, w, x):
    t, tm = x.shape[0], TILES['out_tm']
    yspec = pl.BlockSpec((tm, GW), lambda i: (i, 0))
    xspec = pl.BlockSpec((tm, D_MODEL), lambda i: (i, 0))
    return pl.pallas_call(
        _outproj_kernel,
        grid=(t // tm,),
        in_specs=[yspec, yspec, yspec, yspec, _full((D_MODEL, D_MODEL)), xspec],
        out_specs=xspec,
        out_shape=jax.ShapeDtypeStruct((t, D_MODEL), F32),
        compiler_params=_cparams(("parallel",)),
        name="outproj",
    )(*ys, w, x)


def _mlp(x, g, w1, w2, g_final, final, next_f32, nxt):
    t, tm, tf, ns = x.shape[0], TILES['mlp_tm'], TILES['mlp_tf'], TILES['mlp_slabs']
    nk = D_FF // tf
    xspec = pl.BlockSpec((tm, D_MODEL), lambda i, k: (i, 0))
    in_specs = [xspec, _row(D_MODEL), pl.BlockSpec((D_MODEL, tf), lambda i, k: (0, k)),
                pl.BlockSpec((tf, D_MODEL), lambda i, k: (k, 0)), _row(D_MODEL)]
    out_specs = [xspec]
    out_shape = [jax.ShapeDtypeStruct((t, D_MODEL), F32)]

    def slab(i, k):
        return jnp.minimum(i * nk + k, ns - 1)

    for w in next_f32:
        rows, cols = w.shape[1] // ns, w.shape[2]
        in_specs.append(pl.BlockSpec((None, rows, cols), lambda i, k: (nxt, slab(i, k), 0)))
        out_specs.append(pl.BlockSpec((rows, cols), lambda i, k: (slab(i, k), 0)))
        out_shape.append(jax.ShapeDtypeStruct(w.shape[1:], BF16))
    args = [x, g, w1, w2, g_final, *next_f32]
    return pl.pallas_call(
        functools.partial(_mlp_kernel, final=final, n_cast=len(next_f32)),
        grid=(t // tm, nk),
        in_specs=in_specs,
        out_specs=out_specs,
        out_shape=out_shape,
        scratch_shapes=[pltpu.VMEM((tm, D_MODEL), BF16)],
        compiler_params=_cparams(("arbitrary", "arbitrary")),
        name="mlp",
    )(*args)


def _mixer_call(kernel_fn, name, p, t_rows, n_seq, n_blk, rows, row0, col_specs, state_in, params, y_prev,
                state_out, scratch, tables=(), layer=0):
    def rowmap(cb):
        return lambda s, c: (row0 + s * n_blk + c, cb)

    in_specs = [pl.BlockSpec((rows, w), rowmap(cb)) for w, cb in col_specs]
    args = [p] * len(col_specs)
    for arr, blk in state_in:
        nd = len(blk)
        in_specs.append(pl.BlockSpec((None,) + blk, lambda s, c, nd=nd: (layer, s) + (0,) * (nd - 1)))
        args.append(arr)
    for tab, blk in tables:
        in_specs.append(pl.BlockSpec(blk, lambda s, c: (c, 0)))
        args.append(tab)
    for arr in params:
        in_specs.append(pl.BlockSpec((None,) + arr.shape[1:], lambda s, c, nd=arr.ndim: (layer,) + (0,) * (nd - 1)))
        args.append(arr)
    aliases = {}
    if y_prev is not None:
        aliases = {len(args): 0}
        in_specs.append(_any())
        args.append(y_prev)
    out_specs = [pl.BlockSpec((rows, GW), lambda s, c: (row0 + s * n_blk + c, 0))]
    out_shape = [jax.ShapeDtypeStruct((t_rows, GW), BF16)]
    for shape, blk in state_out:
        nd = len(blk)
        out_specs.append(pl.BlockSpec(blk, lambda s, c, nd=nd: (s,) + (0,) * (nd - 1)))
        out_shape.append(jax.ShapeDtypeStruct(shape, F32))
    return pl.pallas_call(
        kernel_fn,
        grid=(n_seq, n_blk),
        in_specs=in_specs,
        out_specs=out_specs,
        out_shape=out_shape,
        scratch_shapes=scratch,
        input_output_aliases=aliases,
        compiler_params=_cparams(("arbitrary", "arbitrary")),
        name=name,
    )(*args)


def _run_ssm(p, dims, l, params, buf_s, h0_s):
    t, bp, lp, bs, ls = dims
    tp = bp * lp
    q, nb, ns = TILES['ssm_q'], TILES['ssm_nb'], TILES['ssm_ns']
    cols = [(SSM_XBC, CB_XBC), (GW, CB_Z), (128, CB_DT)]
    hshape = lambda n, g: [((n, SSM_HEADS, 64, SSM_STATE), (g, SSM_HEADS, 64, SSM_STATE))]
    scratch = lambda slots, rows: [pltpu.VMEM((slots, HIST + rows, SSM_XBC), F32),
                                   pltpu.VMEM((slots, 2, SSM_STATE, 256), F32)]
    y, h_p = _mixer_call(functools.partial(_ssm_kernel, Q=q, NS=nb, has_state=False), "ssm_prompt", p, t, bp,
                         lp // (nb * q), nb * q, 0, cols, [], params, None, hshape(bp, 1), scratch(1, q), layer=l)
    y, h_s = _mixer_call(functools.partial(_ssm_kernel, Q=ls, NS=ns, has_state=True), "ssm_sample", p, t, bs // ns, 1,
                         ns * ls, tp // (ns * ls), cols,
                         [(buf_s, (ns, HIST, SSM_XBC)), (h0_s, (ns, SSM_HEADS, 64, SSM_STATE))],
                         params, y, hshape(bs, ns), scratch(ns, ls), layer=l)
    return y, h_p, h_s


def _run_lru(p, dims, l, params, buf_s, h0_s):
    t, bp, lp, bs, ls = dims
    tp = bp * lp
    r = TILES['lru_r']
    cols = [(2 * GW, CB_LRU)]
    hshape = lambda n: [((n, 1, GW), (1, 1, GW))]
    scratch = lambda rows: [pltpu.VMEM((HIST + rows, GW), F32), pltpu.VMEM((1, GW), F32)]
    y, h_p = _mixer_call(functools.partial(_lru_kernel, R=r, has_state=False), "lru_prompt", p, t, bp, lp // r, r, 0,
                         cols, [], params, None, hshape(bp), scratch(r), layer=l)
    assert ls == HIST
    ts = bs * ls
    y, h_all = _mixer_call(functools.partial(_lru_short_kernel, N=bs), "lru_sample", p, t, 1, 1, ts, tp // ts, cols,
                           [(buf_s[:, None], (1, bs, HIST, GW)), (h0_s[:, None], (1, bs, 1, GW))], params, y,
                           [((1, ts, GW), (1, ts, GW))], [pltpu.VMEM((HIST + 2 * ts, GW), F32)], layer=l)
    return y, h_p[:, 0], h_all.reshape(bs, ls, GW)[:, ls - 1]


def _run_swa(p, dims, l, rope_p, rope_s, sinks, k0_s, v0_s):
    t, bp, lp, bs, ls = dims
    tp = bp * lp
    q, nb, ns = TILES['swa_q'], TILES['swa_nb'], TILES['swa_ns']
    cols = [(GW, CB_Q), (256, CB_KV)]
    cshape = lambda n, g: [((n, WINDOW, 128), (g, WINDOW, 128))] * 2
    scratch = lambda g: [pltpu.VMEM((g, WINDOW, 128), F32), pltpu.VMEM((g, WINDOW, 128), F32)]
    y, k_p, v_p = _mixer_call(functools.partial(_swa_kernel, QB=q, NS=nb, has_state=False), "swa_prompt", p, t, bp,
                              lp // (nb * q), nb * q, 0, cols, [], [sinks], None, cshape(bp, 1), scratch(1),
                              tables=[(tab, (nb * q, GW)) for tab in rope_p], layer=l)
    y, k_s, v_s = _mixer_call(functools.partial(_swa_kernel, QB=ls, NS=ns, has_state=True), "swa_sample", p, t,
                              bs // ns, 1, ns * ls, tp // (ns * ls), cols,
                              [(k0_s, (ns, WINDOW, 128)), (v0_s, (ns, WINDOW, 128))], [sinks], y, cshape(bs, ns),
                              scratch(ns), tables=[(tab, (ls, GW)) for tab in rope_s], layer=l)
    return y, k_p, v_p, k_s, v_s


def _run_rwkv(p, dims, l, params, sh_rows_s, s0_s):
    t, bp, lp, bs, ls = dims
    tp = bp * lp
    c, nsub = TILES['rwkv_c'], TILES['rwkv_sub']
    g, nsub_s = TILES['rwkv_g'], TILES['rwkv_sub_s']
    cols = [(GW, CB_R), (GW, CB_K), (GW, CB_V), (256, CB_LORA)]
    sshape = lambda n, blk: [((n, 8, 64, 64), (blk, 8, 64, 64))]
    scratch = lambda slots: [pltpu.VMEM((slots, 2, 256, 256), F32), pltpu.VMEM((HIST, RWKV_COLS), F32)]
    rows_p = nsub * c
    y, s_p = _mixer_call(functools.partial(_rwkv_kernel, NSUB=nsub, G=1, C=c, has_state=False), "rwkv_prompt", p, t,
                         bp, lp // rows_p, rows_p, 0, cols, [], params, None, sshape(bp, 1), scratch(1), layer=l)
    nseq = nsub_s * g
    rows_s = nseq * ls
    y, s_s = _mixer_call(functools.partial(_rwkv_kernel, NSUB=nsub_s, G=g, C=ls, has_state=True), "rwkv_sample", p, t,
                         bs // nseq, 1, rows_s, tp // rows_s, cols,
                         [(sh_rows_s, (rows_s, RWKV_COLS)), (s0_s, (nseq, 8, 64, 64))], params, y, sshape(bs, nseq),
                         scratch(nseq), layer=l)
    return y, s_p, s_s


def _rope_tables(pos):
    half = ROT_DIM // 2
    inv = ROPE_THETA ** (-(jnp.arange(half, dtype=F32) * 2.0) / ROT_DIM)
    ang = pos.astype(F32)[:, None] * inv
    cos, sin = jnp.cos(ang), jnp.sin(ang)
    n = pos.shape[0]
    zeros = jnp.zeros((n, half), F32)
    rest = SWA_HEADDIM - ROT_DIM
    cos64 = jnp.concatenate([cos, cos, jnp.ones((n, rest), F32)], axis=1)
    sa64 = jnp.concatenate([-sin, zeros, jnp.zeros((n, rest), F32)], axis=1)
    sb64 = jnp.concatenate([zeros, sin, jnp.zeros((n, rest), F32)], axis=1)
    return tuple(jnp.tile(t, (1, GW // SWA_HEADDIM)) for t in (cos64, sa64, sb64))


def _pad_rows_front(x, axis, total):
    pad = [(0, 0)] * x.ndim
    pad[axis] = (total - x.shape[axis], 0)
    return jnp.pad(x, pad)


def kernel(x_prompt, x_sample, state_ssm, state_ssm_conv, state_rwkv, state_rwkv_shift, cache_swa_k,
           cache_swa_v, state_lru, state_lru_conv, norm_mix, w_in, ssm_conv_w, ssm_conv_b, ssm_dt_bias,
           ssm_a_log, ssm_d, ssm_norm, rwkv_mu, rwkv_w0, rwkv_w2, rwkv_a0, rwkv_a2, rwkv_g2, rwkv_k_k,
           rwkv_k_a, rwkv_r_k, rwkv_ln_w, rwkv_ln_b, swa_sinks, lru_conv_w, lru_conv_b, lru_wa, lru_ba,
           lru_wi, lru_bi, lru_lambda, w_out, norm_mlp, mlp_w1, mlp_w2, norm_final):
    bp, lp, _ = x_prompt.shape
    bs, ls, _ = x_sample.shape
    tp, ts = bp * lp, bs * ls
    t = tp + ts
    dims = (t, bp, lp, bs, ls)
    depth = w_in.shape[0]

    w_in_r = _winprep(w_in.astype(BF16))
    w_out_b = w_out[0].astype(BF16)
    w1_b = mlp_w1[0].astype(BF16)
    w2_b = mlp_w2[0].astype(BF16)
    eye8 = jnp.eye(8, dtype=F32)
    lru_wa_d = jnp.einsum('lncd,nm->lncmd', lru_wa, eye8).reshape(depth, GW, GW).astype(BF16)
    lru_wi_d = jnp.einsum('lncd,nm->lncmd', lru_wi, eye8).reshape(depth, GW, GW).astype(BF16)
    z64 = jnp.zeros((depth, 64, GW), F32)
    rwkv_w2_p = jnp.concatenate([rwkv_w2, z64], axis=1).astype(BF16)
    rwkv_a2_p = jnp.concatenate([z64, rwkv_a2], axis=1).astype(BF16)
    rwkv_g2_b = rwkv_g2.astype(BF16)
    dtb_p = jnp.pad(ssm_dt_bias, ((0, 0), (0, 128 - SSM_HEADS)))
    alog_p = jnp.pad(ssm_a_log, ((0, 0), (0, 128 - SSM_HEADS)))
    d_e = jnp.repeat(ssm_d, 64, axis=-1)
    sinks_b = jnp.broadcast_to(swa_sinks[:, :, None], (depth, 8, 128))
    rope_p = _rope_tables(jnp.arange(lp))
    rope_s = _rope_tables(PAST_LEN + jnp.arange(ls))

    def r2(a, l):
        return a[l][None, :]

    def rows(a):
        return a[:, None, :]

    ssm_params = [ssm_conv_w, rows(ssm_conv_b), rows(dtb_p), rows(alog_p), rows(d_e), rows(ssm_norm)]
    rwkv_params = [rows(rwkv_mu), rows(rwkv_w0), rwkv_w2_p, rows(rwkv_a0), rwkv_a2_p, rwkv_g2_b, rows(rwkv_k_k),
                   rows(rwkv_k_a), rwkv_r_k.reshape(depth, 1, GW), rows(rwkv_ln_w), rows(rwkv_ln_b)]
    lru_params = [lru_conv_w, rows(lru_conv_b), lru_wa_d, rows(lru_ba), lru_wi_d, rows(lru_bi), rows(lru_lambda)]

    ssm_buf = _pad_rows_front(state_ssm_conv, 2, HIST)
    lru_buf = _pad_rows_front(state_lru_conv, 2, HIST)
    lru_h0 = state_lru[:, :, None, :]
    sh_rows = jnp.pad(state_rwkv_shift[:, :, None, :], ((0, 0), (0, 0), (0, ls - 1), (0, 0))).reshape(depth, ts, RWKV_COLS)
    swa_k0 = cache_swa_k.reshape(depth, bs, WINDOW, 128)
    swa_v0 = cache_swa_v.reshape(depth, bs, WINDOW, 128)

    x = jnp.concatenate([x_prompt.reshape(tp, D_MODEL), x_sample.reshape(ts, D_MODEL)], axis=0)

    names = ('ssm_p', 'ssm_s', 'conv_p', 'conv_s', 'rwkv_p', 'rwkv_s', 'shift_p', 'shift_s',
             'k_p', 'k_s', 'v_p', 'v_s', 'lru_p', 'lru_s', 'lconv_p', 'lconv_s')
    outs = {n: [] for n in names}
    for l in range(depth):
        p = _inproj(x, r2(norm_mix, l), w_in_r, l)

        y_ssm, hs_p, hs_s = _run_ssm(p, dims, l, ssm_params, ssm_buf, state_ssm)
        y_rwkv, s_p, s_s = _run_rwkv(p, dims, l, rwkv_params, sh_rows, state_rwkv)
        y_swa, k_p, v_p, k_s, v_s = _run_swa(p, dims, l, rope_p, rope_s, sinks_b, swa_k0, swa_v0)
        y_lru, lh_p, lh_s = _run_lru(p, dims, l, lru_params, lru_buf, lru_h0)

        x1 = _outproj([y_ssm, y_rwkv, y_swa, y_lru], w_out_b, x)
        if l + 1 < depth:
            x, w1_b, w2_b, w_out_b = _mlp(x1, r2(norm_mlp, l), w1_b, w2_b, norm_final[None, :], False,
                                          [mlp_w1, mlp_w2, w_out], l + 1)
        else:
            (x,) = _mlp(x1, r2(norm_mlp, l), w1_b, w2_b, norm_final[None, :], True, [], None)

        def tail_p(n, c0, c1):
            return jnp.stack([p[b * lp + lp - n:b * lp + lp, c0:c1] for b in range(bp)], axis=0)

        ps = p[tp:].reshape(bs, ls, PW)
        lora0 = CB_LORA * 256
        outs['ssm_p'].append(hs_p)
        outs['ssm_s'].append(hs_s)
        outs['conv_p'].append(tail_p(3, 0, SSM_XBC))
        outs['conv_s'].append(ps[:, ls - 3:, 0:SSM_XBC])
        outs['rwkv_p'].append(s_p)
        outs['rwkv_s'].append(s_s)
        outs['shift_p'].append(jnp.concatenate([tail_p(1, 5 * GW, 8 * GW), tail_p(1, lora0, lora0 + 256)], axis=-1)[:, 0])
        outs['shift_s'].append(jnp.concatenate([ps[:, ls - 1, 5 * GW:8 * GW], ps[:, ls - 1, lora0:lora0 + 256]], axis=-1))
        outs['k_p'].append(k_p.reshape(bp, WINDOW, 2, SWA_HEADDIM))
        outs['k_s'].append(k_s.reshape(bs, WINDOW, 2, SWA_HEADDIM))
        outs['v_p'].append(v_p.reshape(bp, WINDOW, 2, SWA_HEADDIM))
        outs['v_s'].append(v_s.reshape(bs, WINDOW, 2, SWA_HEADDIM))
        outs['lru_p'].append(lh_p)
        outs['lru_s'].append(lh_s)
        outs['lconv_p'].append(tail_p(3, SSM_XBC, SSM_XBC + GW))
        outs['lconv_s'].append(ps[:, ls - 3:, SSM_XBC:SSM_XBC + GW])

    st = {n: jnp.stack(v, axis=0) for n, v in outs.items()}
    return (x[:tp].reshape(bp, lp, D_MODEL), x[tp:].reshape(bs, ls, D_MODEL),
            st['ssm_p'], st['ssm_s'], st['conv_p'], st['conv_s'], st['rwkv_p'], st['rwkv_s'],
            st['shift_p'], st['shift_s'], st['k_p'], st['k_s'], st['v_p'], st['v_s'],
            st['lru_p'], st['lru_s'], st['lconv_p'], st['lconv_s'])
```
